```python
import jax, jax.numpy as jnp
from jax import lax
import numpy as np

D_MODEL = 1024
BATCH = 8
SEQ = 4096
DEPTH = 1

CHUNK = 64
D_RNN = 1024
RNN_BLOCKS = 16
RNN_BLOCK_DIM = D_RNN // RNN_BLOCKS
CONV_WIDTH = 4
LRU_C = 8.0
N_HEADS = 16
N_KV_HEADS = 4
HEAD_DIM = 64
D_ATTN = N_HEADS * HEAD_DIM
D_KV = N_KV_HEADS * HEAD_DIM
IDX_HEADS = 8
IDX_DIM = 64
TOPK_MAX = 256
D_FF = 4 * D_MODEL
N_BRANCHES = 2
EPS = 1e-6
SPLITS = (D_RNN, D_RNN, D_ATTN, D_KV, D_KV, IDX_HEADS * IDX_DIM, IDX_DIM, IDX_HEADS, N_BRANCHES * D_MODEL)
D_IN = D_RNN + D_RNN + D_ATTN + D_KV + D_KV + IDX_HEADS * IDX_DIM + IDX_DIM + IDX_HEADS + N_BRANCHES * D_MODEL

kernel_name = "hybrid_rglru_dsa_block"


def rms_norm(x, w):
    xf = x.astype(jnp.float32)
    y = xf * lax.rsqrt(jnp.mean(xf * xf, axis=-1, keepdims=True) + EPS)
    return (y * w.astype(jnp.float32)).astype(x.dtype)


def causal_depthwise_conv(x, w, b):
    T = x.shape[1]
    xp = jnp.pad(x, ((0, 0), (CONV_WIDTH - 1, 0), (0, 0)))
    out = b
    for j in range(CONV_WIDTH):
        out = out + xp[:, j:j + T] * w[j]
    return out


def block_diag_linear(x, w, b):
    xb = x.reshape(x.shape[:-1] + (RNN_BLOCKS, RNN_BLOCK_DIM))
    y = jnp.einsum('btnd,nde->btne', xb, w)
    return y.reshape(x.shape) + b


def rg_lru(x, wa, ba, wx, bx, lam):
    r = jax.nn.sigmoid(block_diag_linear(x, wa, ba).astype(jnp.float32))
    i = jax.nn.sigmoid(block_diag_linear(x, wx, bx).astype(jnp.float32))
    log_a = -LRU_C * r * jax.nn.softplus(-lam.astype(jnp.float32))
    a = jnp.exp(log_a)
    mult = jnp.sqrt(-jnp.expm1(2.0 * log_a))
    u = mult * (i * x.astype(jnp.float32))

    def combine(c1, c2):
        a1, b1 = c1
        a2, b2 = c2
        return a1 * a2, a2 * b1 + b2

    _, h = lax.associative_scan(combine, (a, u), axis=1)
    return h.astype(x.dtype)


def dsa_attention(q, k, v, iq, ik, iw):
    B, T = q.shape[0], q.shape[1]
    n_chunks = T // CHUNK
    top_k = min(TOPK_MAX, T // 4)
    group = N_HEADS // N_KV_HEADS
    scale = HEAD_DIM ** -0.5
    idx_scale = (IDX_DIM ** -0.5) * (IDX_HEADS ** -0.5)
    key_pos = jnp.arange(T)

    def one_chunk(c):
        start = c * CHUNK
        qc = lax.dynamic_slice_in_dim(q, start, CHUNK, axis=1)
        iqc = lax.dynamic_slice_in_dim(iq, start, CHUNK, axis=1)
        iwc = lax.dynamic_slice_in_dim(iw, start, CHUNK, axis=1)
        limit = start + CHUNK
        visible = key_pos < limit
        rel = jax.nn.relu(jnp.einsum('bqhd,bsd->bqhs', iqc, ik).astype(jnp.float32))
        score = jnp.einsum('bqhs,bqh->bqs', rel, iwc.astype(jnp.float32)) * idx_scale
        score = jnp.where(visible, score, -jnp.inf)
        _, idx = lax.top_k(score, top_k)
        valid = idx < limit
        k_sel = jax.vmap(lambda kb, ib: kb[ib])(k, idx)
        v_sel = jax.vmap(lambda vb, ib: vb[ib])(v, idx)
        qg = qc.reshape(B, CHUNK, N_KV_HEADS, group, HEAD_DIM)
        s = jnp.einsum('bqgrd,bqkgd->bqgrk', qg, k_sel).astype(jnp.float32) * scale
        s = jnp.where(valid[:, :, None, None, :], s, -jnp.inf)
        p = jax.nn.softmax(s, axis=-1).astype(v.dtype)
        o = jnp.einsum('bqgrk,bqkgd->bqgrd', p, v_sel)
        return o.reshape(B, CHUNK, D_ATTN)

    out = lax.map(one_chunk, jnp.arange(n_chunks))
    return out.transpose(1, 0, 2, 3).reshape(B, T, D_ATTN)


def hybrid_layer(x, norm1_w, w_in, conv_w, conv_b, rg_wa, rg_ba, rg_wx, rg_bx, rg_lambda,
                 q_norm_w, k_norm_w, idx_k_norm_w, w_o_rnn, w_o_attn, gate_b, w_out,
                 norm2_w, w_ff_in, w_ff_out):
    B, T, _ = x.shape
    h = rms_norm(x, norm1_w)
    u = h @ w_in
    offsets = np.cumsum(SPLITS)[:-1].tolist()
    rnn_x, rnn_gate, q, k, v, iq, ik, iw, g = jnp.split(u, offsets, axis=-1)

    xc = causal_depthwise_conv(rnn_x, conv_w, conv_b)
    y_rnn = rg_lru(xc, rg_wa, rg_ba, rg_wx, rg_bx, rg_lambda) * jax.nn.gelu(rnn_gate)

    q = rms_norm(q.reshape(B, T, N_HEADS, HEAD_DIM), q_norm_w)
    k = rms_norm(k.reshape(B, T, N_KV_HEADS, HEAD_DIM), k_norm_w)
    v = v.reshape(B, T, N_KV_HEADS, HEAD_DIM)
    iq = iq.reshape(B, T, IDX_HEADS, IDX_DIM)
    ik = rms_norm(ik, idx_k_norm_w)
    y_attn = dsa_attention(q, k, v, iq, ik, iw)

    gates = jax.nn.sigmoid((g + gate_b.reshape(-1)).astype(jnp.float32)).astype(x.dtype)
    gates = gates.reshape(B, T, N_BRANCHES, D_MODEL)
    merged = gates[:, :, 0] * (y_rnn @ w_o_rnn) + gates[:, :, 1] * (y_attn @ w_o_attn)
    x = x + merged @ w_out

    h2 = rms_norm(x, norm2_w)
    x = x + jnp.square(jax.nn.relu(h2 @ w_ff_in)) @ w_ff_out
    return x


def setup_inputs(seed: int = 0) -> dict:
    key = jax.random.key(seed)
    ks = jax.random.split(key, 24)
    f32 = jnp.float32

    def nrm(k, shape, scale):
        return jax.random.normal(k, shape, f32) * scale

    a0 = jax.random.uniform(ks[9], (DEPTH, D_RNN), f32, 0.9, 0.999) ** (1.0 / LRU_C)
    rg_lambda = jnp.log(a0) - jnp.log1p(-a0)
    return {
        "x": nrm(ks[0], (BATCH, SEQ, D_MODEL), 1.0),
        "norm1_w": 1.0 + nrm(ks[1], (DEPTH, D_MODEL), 0.02),
        "w_in": nrm(ks[2], (DEPTH, D_MODEL, D_IN), D_MODEL ** -0.5),
        "conv_w": nrm(ks[3], (DEPTH, CONV_WIDTH, D_RNN), CONV_WIDTH ** -0.5),
        "conv_b": nrm(ks[4], (DEPTH, D_RNN), 0.02),
        "rg_wa": nrm(ks[5], (DEPTH, RNN_BLOCKS, RNN_BLOCK_DIM, RNN_BLOCK_DIM), RNN_BLOCK_DIM ** -0.5),
        "rg_ba": nrm(ks[6], (DEPTH, D_RNN), 0.02),
        "rg_wx": nrm(ks[7], (DEPTH, RNN_BLOCKS, RNN_BLOCK_DIM, RNN_BLOCK_DIM), RNN_BLOCK_DIM ** -0.5),
        "rg_bx": nrm(ks[8], (DEPTH, D_RNN), 0.02),
        "rg_lambda": rg_lambda,
        "q_norm_w": 1.0 + nrm(ks[10], (DEPTH, HEAD_DIM), 0.02),
        "k_norm_w": 1.0 + nrm(ks[11], (DEPTH, HEAD_DIM), 0.02),
        "idx_k_norm_w": 1.0 + nrm(ks[12], (DEPTH, IDX_DIM), 0.02),
        "w_o_rnn": nrm(ks[13], (DEPTH, D_RNN, D_MODEL), D_RNN ** -0.5),
        "w_o_attn": nrm(ks[14], (DEPTH, D_ATTN, D_MODEL), D_ATTN ** -0.5),
        "gate_b": nrm(ks[15], (DEPTH, N_BRANCHES, D_MODEL), 0.02),
        "w_out": nrm(ks[16], (DEPTH, D_MODEL, D_MODEL), D_MODEL ** -0.5),
        "norm2_w": 1.0 + nrm(ks[17], (DEPTH, D_MODEL), 0.02),
        "w_ff_in": nrm(ks[18], (DEPTH, D_MODEL, D_FF), D_MODEL ** -0.5),
        "w_ff_out": nrm(ks[19], (DEPTH, D_FF, D_MODEL), D_FF ** -0.5),
    }


def reference(x, norm1_w, w_in, conv_w, conv_b, rg_wa, rg_ba, rg_wx, rg_bx, rg_lambda,
              q_norm_w, k_norm_w, idx_k_norm_w, w_o_rnn, w_o_attn, gate_b, w_out,
              norm2_w, w_ff_in, w_ff_out):
    for l in range(DEPTH):
        x = hybrid_layer(x, norm1_w[l], w_in[l], conv_w[l], conv_b[l], rg_wa[l], rg_ba[l],
                         rg_wx[l], rg_bx[l], rg_lambda[l], q_norm_w[l], k_norm_w[l],
                         idx_k_norm_w[l], w_o_rnn[l], w_o_attn[l], gate_b[l], w_out[l],
                         norm2_w[l], w_ff_in[l], w_ff_out[l])
    return x
```

```python
import functools

import jax
import jax.numpy as jnp
from jax import lax
from jax.experimental import pallas as pl
from jax.experimental.pallas import tpu as pltpu

CHUNK = 64
RNN_BLOCKS = 16
CONV_WIDTH = 4
LRU_C = 8.0
N_HEADS = 16
N_KV_HEADS = 4
HEAD_DIM = 64
IDX_HEADS = 8
IDX_DIM = 64
TOPK_MAX = 256
EPS = 1e-6

LANES = 128
SUBLANES = 8
VMEM_LIMIT_BYTES = 56 * 1024 * 1024

INT_MIN = -(2 ** 31)
MASKED = -1e30

BF16 = jnp.bfloat16
F32 = jnp.float32


def _dot(a, b):
    return jnp.dot(a, b, preferred_element_type=F32)


def _dot_nt(a, b):
    return lax.dot_general(a, b, (((1,), (1,)), ((), ())), preferred_element_type=F32)


def _segment_rms_scale(x, seg):
    x2 = x * x
    lane = lax.broadcasted_iota(jnp.int32, x.shape, 1)
    lo = lane < seg
    s_lo = jnp.sum(jnp.where(lo, x2, 0.0), axis=-1, keepdims=True)
    s_hi = jnp.sum(x2, axis=-1, keepdims=True) - s_lo
    r_lo = lax.rsqrt(s_lo * (1.0 / seg) + EPS)
    r_hi = lax.rsqrt(s_hi * (1.0 / seg) + EPS)
    return jnp.where(lo, r_lo, r_hi)


def _proj_kernel(x_ref, n1_ref, wrg_ref, wg_ref, wq_ref, wk_ref, wiq_ref, wik_ref, wvt_ref,
                 wiwt_ref, gb_ref, qnw_ref, knw_ref, iknw_ref,
                 rg_ref, gates_ref, q_ref, klo_ref, khi_ref, vt_ref, iq_ref, iklo_ref, ikhi_ref,
                 iwt_ref, *, idx_scale):
    x = x_ref[...]
    ms = jnp.mean(x * x, axis=-1, keepdims=True)
    h = (x * lax.rsqrt(ms + EPS) * n1_ref[...]).astype(BF16)

    rg_ref[...] = _dot(h, wrg_ref[...]).astype(BF16)
    gates_ref[...] = jax.nn.sigmoid(_dot(h, wg_ref[...]) + gb_ref[...]).astype(BF16)
    iq_ref[...] = _dot(h, wiq_ref[...]).astype(BF16)

    q = _dot(h, wq_ref[...])
    for j in range(q.shape[1] // LANES):
        sl = slice(j * LANES, (j + 1) * LANES)
        qj = q[:, sl]
        q_ref[:, sl] = (qj * _segment_rms_scale(qj, HEAD_DIM) * qnw_ref[...]).astype(BF16)

    k = _dot(h, wk_ref[...])
    lane = lax.broadcasted_iota(jnp.int32, (k.shape[0], LANES), 1)
    lo = lane < HEAD_DIM
    for j in range(k.shape[1] // LANES):
        kj = k[:, j * LANES:(j + 1) * LANES]
        kj = kj * _segment_rms_scale(kj, HEAD_DIM) * knw_ref[...]
        ks = pltpu.roll(kj, HEAD_DIM, axis=1)
        klo_ref[0, 2 * j] = jnp.where(lo, kj, 0.0).astype(BF16)
        khi_ref[0, 2 * j] = jnp.where(lo, 0.0, ks).astype(BF16)
        klo_ref[0, 2 * j + 1] = jnp.where(lo, ks, 0.0).astype(BF16)
        khi_ref[0, 2 * j + 1] = jnp.where(lo, 0.0, kj).astype(BF16)

    ik = _dot(h, wik_ref[...])
    ik_ms = jnp.sum(ik * ik, axis=-1, keepdims=True) * (1.0 / IDX_DIM)
    ik = ik * lax.rsqrt(ik_ms + EPS) * iknw_ref[...]
    iklo_ref[0] = ik.astype(BF16)
    ikhi_ref[0] = pltpu.roll(ik, IDX_DIM, axis=1).astype(BF16)

    vt_ref[0] = _dot_nt(wvt_ref[...], h).astype(BF16)
    iwt_ref[0] = _dot_nt(wiwt_ref[...], h) * idx_scale


def _const_spec(shape):
    nd = len(shape)
    return pl.BlockSpec(shape, lambda *_: (0,) * nd)


def _proj_call(x2d, params, B, T, tm):
    (n1, wrg, wg, wq, wk, wiq, wik, wvt, wiwt, gb, qnw, knw, iknw) = params
    N, D = x2d.shape
    nt = T // tm
    d_rg, d_g, d_q, d_k, d_iq = wrg.shape[1], wg.shape[1], wq.shape[1], wk.shape[1], wiq.shape[1]
    d_v = wvt.shape[0]
    idx_scale = (IDX_DIM ** -0.5) * (IDX_HEADS ** -0.5)

    row = lambda w: pl.BlockSpec((tm, w), lambda b, i: (b * nt + i, 0))
    in_specs = [row(D)] + [_const_spec(p.shape) for p in params]
    out_shape = (
        jax.ShapeDtypeStruct((N, d_rg), BF16),
        jax.ShapeDtypeStruct((N, d_g), BF16),
        jax.ShapeDtypeStruct((N, d_q), BF16),
        jax.ShapeDtypeStruct((B, N_KV_HEADS, T, LANES), BF16),
        jax.ShapeDtypeStruct((B, N_KV_HEADS, T, LANES), BF16),
        jax.ShapeDtypeStruct((B, d_v, T), BF16),
        jax.ShapeDtypeStruct((N, d_iq), BF16),
        jax.ShapeDtypeStruct((B, T, LANES), BF16),
        jax.ShapeDtypeStruct((B, T, LANES), BF16),
        jax.ShapeDtypeStruct((B, IDX_HEADS, T), F32),
    )
    kv_spec = pl.BlockSpec((1, N_KV_HEADS, tm, LANES), lambda b, i: (b, 0, i, 0))
    ik_spec = pl.BlockSpec((1, tm, LANES), lambda b, i: (b, i, 0))
    out_specs = (
        row(d_rg), row(d_g), row(d_q), kv_spec, kv_spec,
        pl.BlockSpec((1, d_v, tm), lambda b, i: (b, 0, i)),
        row(d_iq), ik_spec, ik_spec,
        pl.BlockSpec((1, IDX_HEADS, tm), lambda b, i: (b, 0, i)),
    )
    return pl.pallas_call(
        functools.partial(_proj_kernel, idx_scale=idx_scale),
        grid=(B, nt),
        in_specs=in_specs,
        out_specs=out_specs,
        out_shape=out_shape,
        compiler_params=pltpu.CompilerParams(
            dimension_semantics=("arbitrary", "arbitrary"),
            vmem_limit_bytes=VMEM_LIMIT_BYTES),
        name="proj",
    )(x2d, *params)


def _shift_rows(x, s, fill):
    rolled = pltpu.roll(x, s, axis=0)
    rows = lax.broadcasted_iota(jnp.int32, x.shape, 0)
    return jnp.where(rows >= s, rolled, fill)


def _linear_scan(a, u):
    tb = a.shape[0]
    s = 1
    while s < tb:
        if s < SUBLANES:
            a_s = _shift_rows(a, s, 1.0)
            u_s = _shift_rows(u, s, 0.0)
            u = a * u_s + u
            a = a * a_s
        else:
            u = jnp.concatenate([u[:s], a[s:] * u[:-s] + u[s:]], axis=0)
            a = jnp.concatenate([a[:s], a[s:] * a[:-s]], axis=0)
        s *= 2
    return a, u


def _rnn_kernel(rg_ref, cw_ref, cb_ref, wd_ref, ba_ref, bx_ref, lam_ref, y_ref,
                xbuf_ref, hc_ref):
    tb = y_ref.shape[0]
    d = y_ref.shape[1]
    halo = SUBLANES

    @pl.when(pl.program_id(1) == 0)
    def _():
        xbuf_ref[0:halo, :] = jnp.zeros((halo, d), F32)
        hc_ref[...] = jnp.zeros_like(hc_ref)

    x = rg_ref[:, 0:d].astype(F32)
    xbuf_ref[halo:halo + tb, :] = x
    xc = cb_ref[...] + cw_ref[CONV_WIDTH - 1:CONV_WIDTH, :] * x
    for j in range(CONV_WIDTH - 1):
        off = halo - (CONV_WIDTH - 1) + j
        xc = xc + cw_ref[j:j + 1, :] * xbuf_ref[off:off + tb, :]
    xbuf_ref[0:halo, :] = x[tb - halo:, :]

    xcb = xc.astype(BF16)
    n_slab = wd_ref.shape[0]
    sw = d // n_slab
    pre_a, pre_i = [], []
    for j in range(n_slab):
        ga = _dot(xcb[:, j * sw:(j + 1) * sw], wd_ref[j])
        pre_a.append(ga[:, :sw])
        pre_i.append(ga[:, sw:])
    r = jax.nn.sigmoid(jnp.concatenate(pre_a, axis=1) + ba_ref[...])
    ig = jax.nn.sigmoid(jnp.concatenate(pre_i, axis=1) + bx_ref[...])
    log_a = (-LRU_C) * r * jax.nn.softplus(-lam_ref[...])
    a = jnp.exp(log_a)
    mult = jnp.sqrt(1.0 - a * a)
    u = mult * (ig * xc)

    a_cum, h = _linear_scan(a, u)
    h = h + a_cum * hc_ref[...]
    hc_ref[...] = h[tb - 1:tb, :]

    gate = rg_ref[:, d:2 * d].astype(F32)
    y_ref[...] = (h * jax.nn.gelu(gate)).astype(BF16)


def _rnn_call(rg, params, B, T, tb):
    cw, cb, wd, ba, bx, lam = params
    N = rg.shape[0]
    d = rg.shape[1] // 2
    nt = T // tb
    return pl.pallas_call(
        _rnn_kernel,
        grid=(B, nt),
        in_specs=[pl.BlockSpec((tb, 2 * d), lambda b, i: (b * nt + i, 0))]
        + [_const_spec(p.shape) for p in params],
        out_specs=pl.BlockSpec((tb, d), lambda b, i: (b * nt + i, 0)),
        out_shape=jax.ShapeDtypeStruct((N, d), BF16),
        scratch_shapes=[pltpu.VMEM((tb + SUBLANES, d), F32), pltpu.VMEM((1, d), F32)],
        compiler_params=pltpu.CompilerParams(
            dimension_semantics=("arbitrary", "arbitrary"),
            vmem_limit_bytes=VMEM_LIMIT_BYTES),
        name="rglru",
    )(rg, *params)


def _sortable_key(s):
    bits = pltpu.bitcast(s, jnp.int32)
    return bits ^ ((bits >> 31) & jnp.int32(0x7FFFFFFF))


def _attn_kernel(q_ref, iq_ref, iwt_ref, klo_ref, khi_ref, vt_ref, iklo_ref, ikhi_ref, o_ref,
                 keys_ref, m_ref, l_ref, acc_ref, *, top_k):
    tq = q_ref.shape[0]
    tk = tq
    qi = pl.program_id(1)
    n_pairs = q_ref.shape[1] // LANES
    group = N_HEADS // N_KV_HEADS

    def idx_block(kb, diagonal):
        ks = pl.multiple_of(kb * tk, tk)
        ik_lo = iklo_ref[0, pl.ds(ks, tk), :]
        ik_hi = ikhi_ref[0, pl.ds(ks, tk), :]
        score = jnp.zeros((tk, tq), F32)
        for p in range(iq_ref.shape[1] // LANES):
            iqp = iq_ref[:, p * LANES:(p + 1) * LANES]
            score = score + jax.nn.relu(_dot_nt(ik_lo, iqp)) * iwt_ref[0, 2 * p:2 * p + 1, :]
            score = score + jax.nn.relu(_dot_nt(ik_hi, iqp)) * iwt_ref[0, 2 * p + 1:2 * p + 2, :]
        key = _sortable_key(score)
        if diagonal:
            kc = lax.broadcasted_iota(jnp.int32, (tk, tq), 0) // CHUNK
            qc = lax.broadcasted_iota(jnp.int32, (tk, tq), 1) // CHUNK
            key = jnp.where(kc <= qc, key, INT_MIN)
        keys_ref[pl.ds(ks, tk), :] = key

    def idx_body(kb, carry):
        idx_block(kb, False)
        return carry

    lax.fori_loop(0, qi, idx_body, 0)
    idx_block(qi, True)

    def count_ge(cand):
        def body(kb, cnt):
            ks = pl.multiple_of(kb * tk, tk)
            hit = jnp.where(keys_ref[pl.ds(ks, tk), :] >= cand, 1, 0)
            return cnt + jnp.sum(hit.reshape(tk // SUBLANES, SUBLANES, tq), axis=0)
        cnt = lax.fori_loop(0, qi + 1, body, jnp.zeros((SUBLANES, tq), jnp.int32))
        return jnp.sum(cnt, axis=0, keepdims=True)

    def bit_body(i, thr):
        bit = 31 - i
        step = jnp.left_shift(jnp.int32(1), bit)
        cand = jnp.where(i == 0, thr ^ step, thr | step)
        return jnp.where(count_ge(cand) >= top_k, cand, thr)

    thr = lax.fori_loop(0, 32, bit_body, jnp.full((1, tq), INT_MIN, jnp.int32))
    thr = jnp.maximum(thr, INT_MIN + 1)

    m_ref[...] = jnp.full(m_ref.shape, MASKED, F32)
    l_ref[...] = jnp.zeros(l_ref.shape, F32)
    acc_ref[...] = jnp.zeros(acc_ref.shape, F32)

    def att_body(kb, carry):
        ks = pl.multiple_of(kb * tk, tk)
        bias = jnp.where(keys_ref[pl.ds(ks, tk), :] >= thr, 0.0, MASKED)
        for j in range(n_pairs):
            qp = q_ref[:, j * LANES:(j + 1) * LANES]
            g = (2 * j) // group
            vt = vt_ref[0, g * HEAD_DIM:(g + 1) * HEAD_DIM, pl.ds(ks, tk)]
            for half in range(2):
                h = 2 * j + half
                kx = (klo_ref if half == 0 else khi_ref)[0, g, pl.ds(ks, tk), :]
                s = _dot_nt(kx, qp) + bias
                m_old = m_ref[h:h + 1, :]
                m_new = jnp.maximum(m_old, jnp.max(s, axis=0, keepdims=True))
                alpha = jnp.exp(m_old - m_new)
                p = jnp.exp(s - m_new)
                l_ref[h:h + 1, :] = alpha * l_ref[h:h + 1, :] + jnp.sum(p, axis=0, keepdims=True)
                m_ref[h:h + 1, :] = m_new
                rows = slice(half * HEAD_DIM, (half + 1) * HEAD_DIM)
                acc_ref[j, rows, :] = alpha * acc_ref[j, rows, :] + _dot(vt, p.astype(BF16))
        return carry

    lax.fori_loop(0, qi + 1, att_body, 0)

    for j in range(n_pairs):
        inv = jnp.concatenate(
            [jnp.broadcast_to(1.0 / l_ref[2 * j + half:2 * j + half + 1, :], (HEAD_DIM, tq))
             for half in range(2)], axis=0)
        o_ref[:, j * LANES:(j + 1) * LANES] = (acc_ref[j] * inv).T.astype(BF16)


def _attn_call(q, iq, iwt, klo, khi, vt, iklo, ikhi, B, T, tq):
    N, d_q = q.shape
    nq = T // tq
    top_k = min(TOPK_MAX, T // 4)
    row = lambda w: pl.BlockSpec((tq, w), lambda b, i: (b * nq + i, 0))
    per_batch = lambda shape: pl.BlockSpec((1,) + shape[1:], lambda b, i: (b,) + (0,) * (len(shape) - 1))
    return pl.pallas_call(
        functools.partial(_attn_kernel, top_k=top_k),
        grid=(B, nq),
        in_specs=[row(d_q), row(iq.shape[1]),
                  pl.BlockSpec((1, IDX_HEADS, tq), lambda b, i: (b, 0, i)),
                  per_batch(klo.shape), per_batch(khi.shape), per_batch(vt.shape),
                  per_batch(iklo.shape), per_batch(ikhi.shape)],
        out_specs=row(d_q),
        out_shape=jax.ShapeDtypeStruct((N, d_q), BF16),
        scratch_shapes=[pltpu.VMEM((T, tq), jnp.int32),
                        pltpu.VMEM((N_HEADS, tq), F32),
                        pltpu.VMEM((N_HEADS, tq), F32),
                        pltpu.VMEM((N_HEADS // 2, LANES, tq), F32)],
        compiler_params=pltpu.CompilerParams(
            dimension_semantics=("arbitrary", "arbitrary"),
            vmem_limit_bytes=VMEM_LIMIT_BYTES),
        name="dsa",
    )(q, iq, iwt, klo, khi, vt, iklo, ikhi)


def _mlp_kernel(x_ref, gates_ref, yr_ref, ya_ref, wor_ref, woa_ref, wout_ref, n2_ref,
                wfi_ref, wfo_ref, o_ref, *, ff_chunk):
    d = x_ref.shape[1]
    merged = (gates_ref[:, 0:d].astype(F32) * _dot(yr_ref[...], wor_ref[...])
              + gates_ref[:, d:2 * d].astype(F32) * _dot(ya_ref[...], woa_ref[...]))
    x1 = x_ref[...] + _dot(merged.astype(BF16), wout_ref[...])
    ms = jnp.mean(x1 * x1, axis=-1, keepdims=True)
    h2 = (x1 * lax.rsqrt(ms + EPS) * n2_ref[...]).astype(BF16)
    acc = x1
    for c in range(wfi_ref.shape[1] // ff_chunk):
        sl = slice(c * ff_chunk, (c + 1) * ff_chunk)
        hid = jnp.square(jax.nn.relu(_dot(h2, wfi_ref[:, sl])))
        acc = acc + _dot(hid.astype(BF16), wfo_ref[sl, :])
    o_ref[...] = acc


def _mlp_call(x2d, gates, y_rnn, y_attn, params, tm):
    N, D = x2d.shape
    row = lambda w: pl.BlockSpec((tm, w), lambda i: (i, 0))
    return pl.pallas_call(
        functools.partial(_mlp_kernel, ff_chunk=1024),
        grid=(N // tm,),
        in_specs=[row(D), row(gates.shape[1]), row(D), row(D)]
        + [_const_spec(p.shape) for p in params],
        out_specs=row(D),
        out_shape=jax.ShapeDtypeStruct((N, D), F32),
        compiler_params=pltpu.CompilerParams(
            dimension_semantics=("arbitrary",),
            vmem_limit_bytes=VMEM_LIMIT_BYTES),
        name="merge_mlp",
    )(x2d, gates, y_rnn, y_attn, *params)


def _block_diag_slabs(wa, wx, slab):
    nb, bd, _ = wa.shape
    per = slab // bd

    def diag(w):
        w = w.reshape(nb // per, per, bd, bd)
        eye = jnp.eye(per, dtype=w.dtype)
        return jnp.einsum('spde,pq->spdqe', w, eye).reshape(nb // per, slab, slab)

    return jnp.concatenate([diag(wa), diag(wx)], axis=-1)


def _layer(x, norm1_w, w_in, conv_w, conv_b, rg_wa, rg_ba, rg_wx, rg_bx, rg_lambda,
           q_norm_w, k_norm_w, idx_k_norm_w, w_o_rnn, w_o_attn, gate_b, w_out,
           norm2_w, w_ff_in, w_ff_out):
    B, T, D = x.shape
    d_rnn = conv_w.shape[1]
    d_attn = N_HEADS * HEAD_DIM
    d_kv = N_KV_HEADS * HEAD_DIM
    d_iq = IDX_HEADS * IDX_DIM
    sizes = (d_rnn, d_rnn, d_attn, d_kv, d_kv, d_iq, IDX_DIM, IDX_HEADS, 2 * D)
    offs = [0]
    for s in sizes:
        offs.append(offs[-1] + s)
    col = lambda i: w_in[:, offs[i]:offs[i + 1]]
    row = lambda v: v.reshape(1, -1).astype(F32)

    proj_params = (
        row(norm1_w),
        jnp.concatenate([col(0), col(1)], axis=1).astype(BF16),
        col(8).astype(BF16),
        col(2).astype(BF16),
        col(3).astype(BF16),
        col(5).astype(BF16),
        jnp.pad(col(6), ((0, 0), (0, LANES - IDX_DIM))).astype(BF16),
        col(4).T.astype(BF16),
        col(7).T.astype(BF16),
        row(gate_b),
        row(jnp.tile(q_norm_w, LANES // HEAD_DIM)) * (HEAD_DIM ** -0.5),
        row(jnp.tile(k_norm_w, LANES // HEAD_DIM)),
        row(jnp.pad(idx_k_norm_w, (0, LANES - IDX_DIM))),
    )
    x2d = x.reshape(B * T, D)
    rg, gates, q, klo, khi, vt, iq, iklo, ikhi, iwt = _proj_call(x2d, proj_params, B, T, tm=512)

    rnn_params = (conv_w.astype(F32), row(conv_b),
                  _block_diag_slabs(rg_wa, rg_wx, 2 * LANES).astype(BF16),
                  row(rg_ba), row(rg_bx), row(rg_lambda))
    y_rnn = _rnn_call(rg, rnn_params, B, T, tb=256)

    y_attn = _attn_call(q, iq, iwt, klo, khi, vt, iklo, ikhi, B, T, tq=256)

    mlp_params = (w_o_rnn.astype(BF16), w_o_attn.astype(BF16), w_out.astype(BF16), row(norm2_w),
                  w_ff_in.astype(BF16), w_ff_out.astype(BF16))
    out = _mlp_call(x2d, gates, y_rnn, y_attn, mlp_params, tm=256)
    return out.reshape(B, T, D)


def kernel(x, norm1_w, w_in, conv_w, conv_b, rg_wa, rg_ba, rg_wx, rg_bx, rg_lambda, q_norm_w,
           k_norm_w, idx_k_norm_w, w_o_rnn, w_o_attn, gate_b, w_out, norm2_w, w_ff_in, w_ff_out):
    for l in range(norm1_w.shape[0]):
        x = _layer(x, norm1_w[l], w_in[l], conv_w[l], conv_b[l], rg_wa[l], rg_ba[l], rg_wx[l],
                   rg_bx[l], rg_lambda[l], q_norm_w[l], k_norm_w[l], idx_k_norm_w[l], w_o_rnn[l],
                   w_o_attn[l], gate_b[l], w_out[l], norm2_w[l], w_ff_in[l], w_ff_out[l])
    return x
```

```python
import functools

import jax
import jax.numpy as jnp
from jax import lax
from jax.experimental import pallas as pl
from jax.experimental.pallas import tpu as pltpu

CHUNK = 64
RNN_BLOCKS = 16
CONV_WIDTH = 4
LRU_C = 8.0
N_HEADS = 16
N_KV_HEADS = 4
HEAD_DIM = 64
IDX_HEADS = 8
IDX_DIM = 64
TOPK_MAX = 256
EPS = 1e-6

LANES = 128
SUBLANES = 8
VMEM_LIMIT_BYTES = 56 * 1024 * 1024

MASKED = -1e30
LOG2_E = 1.4426950408889634

BF16 = jnp.bfloat16
F32 = jnp.float32


def _dot(a, b):
    return jnp.dot(a, b, preferred_element_type=F32)


def _dot_nt(a, b):
    return lax.dot_general(a, b, (((1,), (1,)), ((), ())), preferred_element_type=F32)


def _segment_rms_scale(x, seg):
    x2 = x * x
    lane = lax.broadcasted_iota(jnp.int32, x.shape, 1)
    lo = lane < seg
    s_lo = jnp.sum(jnp.where(lo, x2, 0.0), axis=-1, keepdims=True)
    s_hi = jnp.sum(x2, axis=-1, keepdims=True) - s_lo
    r_lo = lax.rsqrt(s_lo * (1.0 / seg) + EPS)
    r_hi = lax.rsqrt(s_hi * (1.0 / seg) + EPS)
    return jnp.where(lo, r_lo, r_hi)


def _proj_kernel(x_ref, n1_ref, wrg_ref, wg_ref, wq_ref, wk_ref, wiq_ref, wik_ref, wvt_ref,
                 wiwt_ref, gb_ref, qnw_ref, knw_ref, iknw_ref,
                 rg_ref, gates_ref, q_ref, klo_ref, khi_ref, vt_ref, iq_ref, iklo_ref, ikhi_ref,
                 iwt_ref, *, idx_scale):
    x = x_ref[...]
    ms = jnp.mean(x * x, axis=-1, keepdims=True)
    h = (x * lax.rsqrt(ms + EPS) * n1_ref[...]).astype(BF16)

    rg_ref[...] = _dot(h, wrg_ref[...]).astype(BF16)
    gates_ref[...] = jax.nn.sigmoid(_dot(h, wg_ref[...]) + gb_ref[...]).astype(BF16)
    iq_ref[...] = _dot(h, wiq_ref[...]).astype(BF16)

    q = _dot(h, wq_ref[...])
    for j in range(q.shape[1] // LANES):
        sl = slice(j * LANES, (j + 1) * LANES)
        qj = q[:, sl]
        q_ref[:, sl] = (qj * _segment_rms_scale(qj, HEAD_DIM) * qnw_ref[...]).astype(BF16)

    k = _dot(h, wk_ref[...])
    lane = lax.broadcasted_iota(jnp.int32, (k.shape[0], LANES), 1)
    lo = lane < HEAD_DIM
    for j in range(k.shape[1] // LANES):
        kj = k[:, j * LANES:(j + 1) * LANES]
        kj = kj * _segment_rms_scale(kj, HEAD_DIM) * knw_ref[...]
        ks = pltpu.roll(kj, HEAD_DIM, axis=1)
        klo_ref[0, 2 * j] = jnp.where(lo, kj, 0.0).astype(BF16)
        khi_ref[0, 2 * j] = jnp.where(lo, 0.0, ks).astype(BF16)
        klo_ref[0, 2 * j + 1] = jnp.where(lo, ks, 0.0).astype(BF16)
        khi_ref[0, 2 * j + 1] = jnp.where(lo, 0.0, kj).astype(BF16)

    ik = _dot(h, wik_ref[...])
    ik_ms = jnp.sum(ik * ik, axis=-1, keepdims=True) * (1.0 / IDX_DIM)
    ik = ik * lax.rsqrt(ik_ms + EPS) * iknw_ref[...]
    iklo_ref[0] = ik.astype(BF16)
    ikhi_ref[0] = pltpu.roll(ik, IDX_DIM, axis=1).astype(BF16)

    vt_ref[0] = _dot_nt(wvt_ref[...], h).astype(BF16)
    iwt_ref[0] = _dot_nt(wiwt_ref[...], h) * idx_scale


def _const_spec(shape):
    nd = len(shape)
    return pl.BlockSpec(shape, lambda *_: (0,) * nd)


def _proj_call(x2d, params, B, T, tm):
    (n1, wrg, wg, wq, wk, wiq, wik, wvt, wiwt, gb, qnw, knw, iknw) = params
    N, D = x2d.shape
    nt = T // tm
    d_rg, d_g, d_q, d_k, d_iq = wrg.shape[1], wg.shape[1], wq.shape[1], wk.shape[1], wiq.shape[1]
    d_v = wvt.shape[0]
    idx_scale = (IDX_DIM ** -0.5) * (IDX_HEADS ** -0.5)

    row = lambda w: pl.BlockSpec((tm, w), lambda b, i: (b * nt + i, 0))
    in_specs = [row(D)] + [_const_spec(p.shape) for p in params]
    out_shape = (
        jax.ShapeDtypeStruct((N, d_rg), BF16),
        jax.ShapeDtypeStruct((N, d_g), BF16),
        jax.ShapeDtypeStruct((N, d_q), BF16),
        jax.ShapeDtypeStruct((B, N_KV_HEADS, T, LANES), BF16),
        jax.ShapeDtypeStruct((B, N_KV_HEADS, T, LANES), BF16),
        jax.ShapeDtypeStruct((B, d_v, T), BF16),
        jax.ShapeDtypeStruct((N, d_iq), BF16),
        jax.ShapeDtypeStruct((B, T, LANES), BF16),
        jax.ShapeDtypeStruct((B, T, LANES), BF16),
        jax.ShapeDtypeStruct((B, IDX_HEADS, T), F32),
    )
    kv_spec = pl.BlockSpec((1, N_KV_HEADS, tm, LANES), lambda b, i: (b, 0, i, 0))
    ik_spec = pl.BlockSpec((1, tm, LANES), lambda b, i: (b, i, 0))
    out_specs = (
        row(d_rg), row(d_g), row(d_q), kv_spec, kv_spec,
        pl.BlockSpec((1, d_v, tm), lambda b, i: (b, 0, i)),
        row(d_iq), ik_spec, ik_spec,
        pl.BlockSpec((1, IDX_HEADS, tm), lambda b, i: (b, 0, i)),
    )
    return pl.pallas_call(
        functools.partial(_proj_kernel, idx_scale=idx_scale),
        grid=(B, nt),
        in_specs=in_specs,
        out_specs=out_specs,
        out_shape=out_shape,
        compiler_params=pltpu.CompilerParams(
            dimension_semantics=("arbitrary", "arbitrary"),
            vmem_limit_bytes=VMEM_LIMIT_BYTES),
        name="proj",
    )(x2d, *params)


def _shift_rows(x, s, fill):
    rolled = pltpu.roll(x, s, axis=0)
    rows = lax.broadcasted_iota(jnp.int32, x.shape, 0)
    return jnp.where(rows >= s, rolled, fill)


def _linear_scan(a, u):
    tb = a.shape[0]
    s = 1
    while s < tb:
        if s < SUBLANES:
            a_s = _shift_rows(a, s, 1.0)
            u_s = _shift_rows(u, s, 0.0)
            u = a * u_s + u
            a = a * a_s
        else:
            u = jnp.concatenate([u[:s], a[s:] * u[:-s] + u[s:]], axis=0)
            a = jnp.concatenate([a[:s], a[s:] * a[:-s]], axis=0)
        s *= 2
    return a, u


def _rnn_kernel(rg_ref, cw_ref, cb_ref, wd_ref, ba_ref, bx_ref, lam_ref, y_ref,
                xbuf_ref, hc_ref):
    tb = y_ref.shape[0]
    d = y_ref.shape[1]
    halo = SUBLANES

    @pl.when(pl.program_id(1) == 0)
    def _():
        xbuf_ref[0:halo, :] = jnp.zeros((halo, d), F32)
        hc_ref[...] = jnp.zeros_like(hc_ref)

    x = rg_ref[:, 0:d].astype(F32)
    xbuf_ref[halo:halo + tb, :] = x
    xc = cb_ref[...] + cw_ref[CONV_WIDTH - 1:CONV_WIDTH, :] * x
    for j in range(CONV_WIDTH - 1):
        off = halo - (CONV_WIDTH - 1) + j
        xc = xc + cw_ref[j:j + 1, :] * xbuf_ref[off:off + tb, :]
    xbuf_ref[0:halo, :] = x[tb - halo:, :]

    xcb = xc.astype(BF16)
    n_slab = wd_ref.shape[0]
    sw = d // n_slab
    pre_a, pre_i = [], []
    for j in range(n_slab):
        ga = _dot(xcb[:, j * sw:(j + 1) * sw], wd_ref[j])
        pre_a.append(ga[:, :sw])
        pre_i.append(ga[:, sw:])
    r = jax.nn.sigmoid(jnp.concatenate(pre_a, axis=1) + ba_ref[...])
    ig = jax.nn.sigmoid(jnp.concatenate(pre_i, axis=1) + bx_ref[...])
    log_a = (-LRU_C) * r * jax.nn.softplus(-lam_ref[...])
    a = jnp.exp(log_a)
    mult = jnp.sqrt(1.0 - a * a)
    u = mult * (ig * xc)

    a_cum, h = _linear_scan(a, u)
    h = h + a_cum * hc_ref[...]
    hc_ref[...] = h[tb - 1:tb, :]

    gate = rg_ref[:, d:2 * d].astype(F32)
    y_ref[...] = (h * jax.nn.gelu(gate)).astype(BF16)


def _rnn_call(rg, params, B, T, tb):
    cw, cb, wd, ba, bx, lam = params
    N = rg.shape[0]
    d = rg.shape[1] // 2
    nt = T // tb
    return pl.pallas_call(
        _rnn_kernel,
        grid=(B, nt),
        in_specs=[pl.BlockSpec((tb, 2 * d), lambda b, i: (b * nt + i, 0))]
        + [_const_spec(p.shape) for p in params],
        out_specs=pl.BlockSpec((tb, d), lambda b, i: (b * nt + i, 0)),
        out_shape=jax.ShapeDtypeStruct((N, d), BF16),
        scratch_shapes=[pltpu.VMEM((tb + SUBLANES, d), F32), pltpu.VMEM((1, d), F32)],
        compiler_params=pltpu.CompilerParams(
            dimension_semantics=("arbitrary", "arbitrary"),
            vmem_limit_bytes=VMEM_LIMIT_BYTES),
        name="rglru",
    )(rg, *params)


def _fold_rows(x, op):
    return op(x.reshape(x.shape[0] // SUBLANES, SUBLANES, x.shape[1]), axis=0)


def _attn_kernel(q_ref, iq_ref, iwt_ref, klo_ref, khi_ref, vt_ref, iklo_ref, ikhi_ref, o_ref,
                 sc_ref, bias_ref, s_ref, p_ref, alpha_ref, m_ref, l_ref, acc_ref, *, top_k):
    tq = q_ref.shape[0]
    tk = tq
    qi = pl.program_id(1)
    n_pairs = q_ref.shape[1] // LANES
    group = N_HEADS // N_KV_HEADS
    n_sub = tk // CHUNK

    def idx_block(kb, diagonal):
        ks = pl.multiple_of(kb * tk, tk)
        ik_lo = iklo_ref[0, pl.ds(ks, tk), :]
        ik_hi = ikhi_ref[0, pl.ds(ks, tk), :]
        score = jnp.zeros((tk, tq), F32)
        for p in range(iq_ref.shape[1] // LANES):
            iqp = iq_ref[:, p * LANES:(p + 1) * LANES]
            score = score + jax.nn.relu(_dot_nt(ik_lo, iqp)) * iwt_ref[0, 2 * p:2 * p + 1, :]
            score = score + jax.nn.relu(_dot_nt(ik_hi, iqp)) * iwt_ref[0, 2 * p + 1:2 * p + 2, :]
        if diagonal:
            kc = lax.broadcasted_iota(jnp.int32, (tk, tq), 0) // CHUNK
            qc = lax.broadcasted_iota(jnp.int32, (tk, tq), 1) // CHUNK
            visible = kc <= qc
            sc_ref[pl.ds(ks, tk), :] = jnp.where(visible, score, -jnp.inf)
            return (_fold_rows(jnp.where(visible, score, jnp.inf), jnp.min),
                    _fold_rows(jnp.where(visible, score, -jnp.inf), jnp.max))
        sc_ref[pl.ds(ks, tk), :] = score
        return _fold_rows(score, jnp.min), _fold_rows(score, jnp.max)

    def idx_body(kb, carry):
        mn, mx = idx_block(kb, False)
        return jnp.minimum(carry[0], mn), jnp.maximum(carry[1], mx)

    mn, mx = lax.fori_loop(0, qi, idx_body, (jnp.full((SUBLANES, tq), jnp.inf, F32),
                                             jnp.full((SUBLANES, tq), -jnp.inf, F32)))
    mn_d, mx_d = idx_block(qi, True)
    lo0 = jnp.min(jnp.minimum(mn, mn_d), axis=0, keepdims=True)
    hi0 = jnp.max(jnp.maximum(mx, mx_d), axis=0, keepdims=True)

    def count_ge(t):
        def body(kb, cnt):
            ks = pl.multiple_of(kb * tk, tk)
            hit = jnp.where(sc_ref[pl.ds(ks, tk), :] >= t, 1, 0)
            return cnt + _fold_rows(hit, jnp.sum)
        cnt = lax.fori_loop(0, qi + 1, body, jnp.zeros((SUBLANES, tq), jnp.int32))
        return jnp.sum(cnt, axis=0, keepdims=True)

    def bisect_cond(state):
        return state[3] > 0

    def bisect_body(state):
        lo, hi, c_lo, _ = state
        mid = lo + 0.5 * (hi - lo)
        c = count_ge(mid)
        ok = c >= top_k
        c_lo = jnp.where(ok, c, c_lo)
        moved = (mid > lo) & (mid < hi)
        pending = jnp.where((c_lo > top_k) & moved, 1, 0)
        return jnp.where(ok, mid, lo), jnp.where(ok, hi, mid), c_lo, jnp.max(pending)

    n_visible = qi * tq + (lax.broadcasted_iota(jnp.int32, (1, tq), 1) // CHUNK + 1) * CHUNK
    thr, _, _, _ = lax.while_loop(
        bisect_cond, bisect_body,
        (lo0, hi0, n_visible, jnp.max(jnp.where(n_visible > top_k, 1, 0))))

    m_ref[...] = jnp.full(m_ref.shape, MASKED, F32)
    l_ref[...] = jnp.zeros(l_ref.shape, F32)
    acc_ref[...] = jnp.zeros(acc_ref.shape, F32)

    def att_body(kb, carry):
        ks = pl.multiple_of(kb * tk, tk)
        bias_ref[...] = jnp.where(sc_ref[pl.ds(ks, tk), :] >= thr, 0.0, MASKED)
        for j in range(n_pairs):
            qp = q_ref[:, j * LANES:(j + 1) * LANES]
            g = (2 * j) // group
            for half in range(2):
                kx = (klo_ref if half == 0 else khi_ref)[0, g, pl.ds(ks, tk), :]
                s_ref[2 * j + half] = _dot_nt(kx, qp) + bias_ref[...]
        for h in range(N_HEADS):
            m_old = m_ref[h:h + 1, :]
            mb = _fold_rows(s_ref[h, 0:CHUNK, :], jnp.max)
            for c in range(1, n_sub):
                mb = jnp.maximum(mb, _fold_rows(s_ref[h, c * CHUNK:(c + 1) * CHUNK, :], jnp.max))
            m_new = jnp.maximum(m_old, jnp.max(mb, axis=0, keepdims=True))
            alpha = jnp.exp2(m_old - m_new)
            lsum = jnp.zeros((SUBLANES, tq), F32)
            for c in range(n_sub):
                rows = slice(c * CHUNK, (c + 1) * CHUNK)
                p = jnp.exp2(s_ref[h, rows, :] - m_new)
                lsum = lsum + _fold_rows(p, jnp.sum)
                p_ref[h, rows, :] = p.astype(BF16)
            l_ref[h:h + 1, :] = alpha * l_ref[h:h + 1, :] + jnp.sum(lsum, axis=0, keepdims=True)
            m_ref[h:h + 1, :] = m_new
            alpha_ref[h:h + 1, :] = alpha
        for j in range(n_pairs):
            g = (2 * j) // group
            vt = vt_ref[0, g * HEAD_DIM:(g + 1) * HEAD_DIM, pl.ds(ks, tk)]
            for half in range(2):
                h = 2 * j + half
                rows = slice(half * HEAD_DIM, (half + 1) * HEAD_DIM)
                acc_ref[j, rows, :] = (alpha_ref[h:h + 1, :] * acc_ref[j, rows, :]
                                       + _dot(vt, p_ref[h]))
        return carry

    lax.fori_loop(0, qi + 1, att_body, 0)

    for j in range(n_pairs):
        inv = jnp.concatenate(
            [jnp.broadcast_to(1.0 / l_ref[2 * j + half:2 * j + half + 1, :], (HEAD_DIM, tq))
             for half in range(2)], axis=0)
        o_ref[:, j * LANES:(j + 1) * LANES] = (acc_ref[j] * inv).T.astype(BF16)


def _attn_call(q, iq, iwt, klo, khi, vt, iklo, ikhi, B, T, tq):
    N, d_q = q.shape
    nq = T // tq
    top_k = min(TOPK_MAX, T // 4)
    row = lambda w: pl.BlockSpec((tq, w), lambda b, i: (b * nq + i, 0))
    per_batch = lambda shape: pl.BlockSpec((1,) + shape[1:], lambda b, i: (b,) + (0,) * (len(shape) - 1))
    return pl.pallas_call(
        functools.partial(_attn_kernel, top_k=top_k),
        grid=(B, nq),
        in_specs=[row(d_q), row(iq.shape[1]),
                  pl.BlockSpec((1, IDX_HEADS, tq), lambda b, i: (b, 0, i)),
                  per_batch(klo.shape), per_batch(khi.shape), per_batch(vt.shape),
                  per_batch(iklo.shape), per_batch(ikhi.shape)],
        out_specs=row(d_q),
        out_shape=jax.ShapeDtypeStruct((N, d_q), BF16),
        scratch_shapes=[pltpu.VMEM((T, tq), F32),
                        pltpu.VMEM((tq, tq), F32),
                        pltpu.VMEM((N_HEADS, tq, tq), F32),
                        pltpu.VMEM((N_HEADS, tq, tq), BF16),
                        pltpu.VMEM((N_HEADS, tq), F32),
                        pltpu.VMEM((N_HEADS, tq), F32),
                        pltpu.VMEM((N_HEADS, tq), F32),
                        pltpu.VMEM((N_HEADS // 2, LANES, tq), F32)],
        compiler_params=pltpu.CompilerParams(
            dimension_semantics=("arbitrary", "arbitrary"),
            vmem_limit_bytes=VMEM_LIMIT_BYTES),
        name="dsa",
    )(q, iq, iwt, klo, khi, vt, iklo, ikhi)


def _mlp_kernel(x_ref, gates_ref, yr_ref, ya_ref, wor_ref, woa_ref, wout_ref, n2_ref,
                wfi_ref, wfo_ref, o_ref, *, ff_chunk):
    d = x_ref.shape[1]
    merged = (gates_ref[:, 0:d].astype(F32) * _dot(yr_ref[...], wor_ref[...])
              + gates_ref[:, d:2 * d].astype(F32) * _dot(ya_ref[...], woa_ref[...]))
    x1 = x_ref[...] + _dot(merged.astype(BF16), wout_ref[...])
    ms = jnp.mean(x1 * x1, axis=-1, keepdims=True)
    h2 = (x1 * lax.rsqrt(ms + EPS) * n2_ref[...]).astype(BF16)
    acc = x1
    for c in range(wfi_ref.shape[1] // ff_chunk):
        sl = slice(c * ff_chunk, (c + 1) * ff_chunk)
        hid = jnp.square(jax.nn.relu(_dot(h2, wfi_ref[:, sl])))
        acc = acc + _dot(hid.astype(BF16), wfo_ref[sl, :])
    o_ref[...] = acc


def _mlp_call(x2d, gates, y_rnn, y_attn, params, tm):
    N, D = x2d.shape
    row = lambda w: pl.BlockSpec((tm, w), lambda i: (i, 0))
    return pl.pallas_call(
        functools.partial(_mlp_kernel, ff_chunk=1024),
        grid=(N // tm,),
        in_specs=[row(D), row(gates.shape[1]), row(D), row(D)]
        + [_const_spec(p.shape) for p in params],
        out_specs=row(D),
        out_shape=jax.ShapeDtypeStruct((N, D), F32),
        compiler_params=pltpu.CompilerParams(
            dimension_semantics=("arbitrary",),
            vmem_limit_bytes=VMEM_LIMIT_BYTES),
        name="merge_mlp",
    )(x2d, gates, y_rnn, y_attn, *params)


def _block_diag_slabs(wa, wx, slab):
    nb, bd, _ = wa.shape
    per = slab // bd

    def diag(w):
        w = w.reshape(nb // per, per, bd, bd)
        eye = jnp.eye(per, dtype=w.dtype)
        return jnp.einsum('spde,pq->spdqe', w, eye).reshape(nb // per, slab, slab)

    return jnp.concatenate([diag(wa), diag(wx)], axis=-1)


def _layer(x, norm1_w, w_in, conv_w, conv_b, rg_wa, rg_ba, rg_wx, rg_bx, rg_lambda,
           q_norm_w, k_norm_w, idx_k_norm_w, w_o_rnn, w_o_attn, gate_b, w_out,
           norm2_w, w_ff_in, w_ff_out):
    B, T, D = x.shape
    d_rnn = conv_w.shape[1]
    d_attn = N_HEADS * HEAD_DIM
    d_kv = N_KV_HEADS * HEAD_DIM
    d_iq = IDX_HEADS * IDX_DIM
    sizes = (d_rnn, d_rnn, d_attn, d_kv, d_kv, d_iq, IDX_DIM, IDX_HEADS, 2 * D)
    offs = [0]
    for s in sizes:
        offs.append(offs[-1] + s)
    col = lambda i: w_in[:, offs[i]:offs[i + 1]]
    row = lambda v: v.reshape(1, -1).astype(F32)

    proj_params = (
        row(norm1_w),
        jnp.concatenate([col(0), col(1)], axis=1).astype(BF16),
        col(8).astype(BF16),
        col(2).astype(BF16),
        col(3).astype(BF16),
        col(5).astype(BF16),
        jnp.pad(col(6), ((0, 0), (0, LANES - IDX_DIM))).astype(BF16),
        col(4).T.astype(BF16),
        col(7).T.astype(BF16),
        row(gate_b),
        row(jnp.tile(q_norm_w, LANES // HEAD_DIM)) * (HEAD_DIM ** -0.5 * LOG2_E),
        row(jnp.tile(k_norm_w, LANES // HEAD_DIM)),
        row(jnp.pad(idx_k_norm_w, (0, LANES - IDX_DIM))),
    )
    x2d = x.reshape(B * T, D)
    rg, gates, q, klo, khi, vt, iq, iklo, ikhi, iwt = _proj_call(x2d, proj_params, B, T, tm=512)

    rnn_params = (conv_w.astype(F32), row(conv_b),
                  _block_diag_slabs(rg_wa, rg_wx, 2 * LANES).astype(BF16),
                  row(rg_ba), row(rg_bx), row(rg_lambda))
    y_rnn = _rnn_call(rg, rnn_params, B, T, tb=256)

    y_attn = _attn_call(q, iq, iwt, klo, khi, vt, iklo, ikhi, B, T, tq=256)

    mlp_params = (w_o_rnn.astype(BF16), w_o_attn.astype(BF16), w_out.astype(BF16), row(norm2_w),
                  w_ff_in.astype(BF16), w_ff_out.astype(BF16))
    out = _mlp_call(x2d, gates, y_rnn, y_attn, mlp_params, tm=256)
    return out.reshape(B, T, D)


def kernel(x, norm1_w, w_in, conv_w, conv_b, rg_wa, rg_ba, rg_wx, rg_bx, rg_lambda, q_norm_w,
           k_norm_w, idx_k_norm_w, w_o_rnn, w_o_attn, gate_b, w_out, norm2_w, w_ff_in, w_ff_out):
    for l in range(norm1_w.shape[0]):
        x = _layer(x, norm1_w[l], w_in[l], conv_w[l], conv_b[l], rg_wa[l], rg_ba[l], rg_wx[l],
                   rg_bx[l], rg_lambda[l], q_norm_w[l], k_norm_w[l], idx_k_norm_w[l], w_o_rnn[l],
                   w_o_attn[l], gate_b[l], w_out[l], norm2_w[l], w_ff_in[l], w_ff_out[l])
    return x
```

```python
import functools

import jax
import jax.numpy as jnp
from jax import lax
from jax.experimental import pallas as pl
from jax.experimental.pallas import tpu as pltpu

CHUNK = 64
RNN_BLOCKS = 16
CONV_WIDTH = 4
LRU_C = 8.0
N_HEADS = 16
N_KV_HEADS = 4
HEAD_DIM = 64
IDX_HEADS = 8
IDX_DIM = 64
TOPK_MAX = 256
EPS = 1e-6

LANES = 128
SUBLANES = 8
VMEM_LIMIT_BYTES = 56 * 1024 * 1024

MASKED = -1e30
LOG2_E = 1.4426950408889634
BISECT_STEPS_PER_ROUND = 4
BISECT_MAX_ROUNDS = 8

BF16 = jnp.bfloat16
F32 = jnp.float32


def _dot(a, b):
    return jnp.dot(a, b, preferred_element_type=F32)


def _dot_nt(a, b):
    return lax.dot_general(a, b, (((1,), (1,)), ((), ())), preferred_element_type=F32)


def _segment_rms_scale(x, seg):
    x2 = x * x
    lane = lax.broadcasted_iota(jnp.int32, x.shape, 1)
    lo = lane < seg
    s_lo = jnp.sum(jnp.where(lo, x2, 0.0), axis=-1, keepdims=True)
    s_hi = jnp.sum(x2, axis=-1, keepdims=True) - s_lo
    r_lo = lax.rsqrt(s_lo * (1.0 / seg) + EPS)
    r_hi = lax.rsqrt(s_hi * (1.0 / seg) + EPS)
    return jnp.where(lo, r_lo, r_hi)


def _proj_kernel(x_ref, n1_ref, wrg_ref, wg_ref, wq_ref, wk_ref, wiq_ref, wik_ref, wvt_ref,
                 wiwt_ref, gb_ref, qnw_ref, knw_ref, iknw_ref,
                 rg_ref, gates_ref, q_ref, klo_ref, khi_ref, vt_ref, iq_ref, iklo_ref, ikhi_ref,
                 iwt_ref, *, idx_scale):
    x = x_ref[...]
    ms = jnp.mean(x * x, axis=-1, keepdims=True)
    h = (x * lax.rsqrt(ms + EPS) * n1_ref[...]).astype(BF16)

    rg_ref[...] = _dot(h, wrg_ref[...]).astype(BF16)
    gates_ref[...] = jax.nn.sigmoid(_dot(h, wg_ref[...]) + gb_ref[...]).astype(BF16)
    iq_ref[...] = _dot(h, wiq_ref[...]).astype(BF16)

    q = _dot(h, wq_ref[...])
    for j in range(q.shape[1] // LANES):
        sl = slice(j * LANES, (j + 1) * LANES)
        qj = q[:, sl]
        q_ref[:, sl] = (qj * _segment_rms_scale(qj, HEAD_DIM) * qnw_ref[...]).astype(BF16)

    k = _dot(h, wk_ref[...])
    lane = lax.broadcasted_iota(jnp.int32, (k.shape[0], LANES), 1)
    lo = lane < HEAD_DIM
    for j in range(k.shape[1] // LANES):
        kj = k[:, j * LANES:(j + 1) * LANES]
        kj = kj * _segment_rms_scale(kj, HEAD_DIM) * knw_ref[...]
        ks = pltpu.roll(kj, HEAD_DIM, axis=1)
        klo_ref[0, 2 * j] = jnp.where(lo, kj, 0.0).astype(BF16)
        khi_ref[0, 2 * j] = jnp.where(lo, 0.0, ks).astype(BF16)
        klo_ref[0, 2 * j + 1] = jnp.where(lo, ks, 0.0).astype(BF16)
        khi_ref[0, 2 * j + 1] = jnp.where(lo, 0.0, kj).astype(BF16)

    ik = _dot(h, wik_ref[...])
    ik_ms = jnp.sum(ik * ik, axis=-1, keepdims=True) * (1.0 / IDX_DIM)
    ik = ik * lax.rsqrt(ik_ms + EPS) * iknw_ref[...]
    iklo_ref[0] = ik.astype(BF16)
    ikhi_ref[0] = pltpu.roll(ik, IDX_DIM, axis=1).astype(BF16)

    vt_ref[0] = _dot_nt(wvt_ref[...], h).astype(BF16)
    iwt_ref[0] = _dot_nt(wiwt_ref[...], h) * idx_scale


def _const_spec(shape):
    nd = len(shape)
    return pl.BlockSpec(shape, lambda *_: (0,) * nd)


def _proj_call(x2d, params, B, T, tm):
    (n1, wrg, wg, wq, wk, wiq, wik, wvt, wiwt, gb, qnw, knw, iknw) = params
    N, D = x2d.shape
    nt = T // tm
    d_rg, d_g, d_q, d_k, d_iq = wrg.shape[1], wg.shape[1], wq.shape[1], wk.shape[1], wiq.shape[1]
    d_v = wvt.shape[0]
    idx_scale = (IDX_DIM ** -0.5) * (IDX_HEADS ** -0.5)

    row = lambda w: pl.BlockSpec((tm, w), lambda b, i: (b * nt + i, 0))
    in_specs = [row(D)] + [_const_spec(p.shape) for p in params]
    out_shape = (
        jax.ShapeDtypeStruct((N, d_rg), BF16),
        jax.ShapeDtypeStruct((N, d_g), BF16),
        jax.ShapeDtypeStruct((N, d_q), BF16),
        jax.ShapeDtypeStruct((B, N_KV_HEADS, T, LANES), BF16),
        jax.ShapeDtypeStruct((B, N_KV_HEADS, T, LANES), BF16),
        jax.ShapeDtypeStruct((B, d_v, T), BF16),
        jax.ShapeDtypeStruct((N, d_iq), BF16),
        jax.ShapeDtypeStruct((B, T, LANES), BF16),
        jax.ShapeDtypeStruct((B, T, LANES), BF16),
        jax.ShapeDtypeStruct((B, IDX_HEADS, T), F32),
    )
    kv_spec = pl.BlockSpec((1, N_KV_HEADS, tm, LANES), lambda b, i: (b, 0, i, 0))
    ik_spec = pl.BlockSpec((1, tm, LANES), lambda b, i: (b, i, 0))
    out_specs = (
        row(d_rg), row(d_g), row(d_q), kv_spec, kv_spec,
        pl.BlockSpec((1, d_v, tm), lambda b, i: (b, 0, i)),
        row(d_iq), ik_spec, ik_spec,
        pl.BlockSpec((1, IDX_HEADS, tm), lambda b, i: (b, 0, i)),
    )
    return pl.pallas_call(
        functools.partial(_proj_kernel, idx_scale=idx_scale),
        grid=(B, nt),
        in_specs=in_specs,
        out_specs=out_specs,
        out_shape=out_shape,
        compiler_params=pltpu.CompilerParams(
            dimension_semantics=("arbitrary", "arbitrary"),
            vmem_limit_bytes=VMEM_LIMIT_BYTES),
        name="proj",
    )(x2d, *params)


def _shift_rows(x, s, fill):
    rolled = pltpu.roll(x, s, axis=0)
    rows = lax.broadcasted_iota(jnp.int32, x.shape, 0)
    return jnp.where(rows >= s, rolled, fill)


def _linear_scan(a, u):
    tb = a.shape[0]
    s = 1
    while s < tb:
        if s < SUBLANES:
            a_s = _shift_rows(a, s, 1.0)
            u_s = _shift_rows(u, s, 0.0)
            u = a * u_s + u
            a = a * a_s
        else:
            u = jnp.concatenate([u[:s], a[s:] * u[:-s] + u[s:]], axis=0)
            a = jnp.concatenate([a[:s], a[s:] * a[:-s]], axis=0)
        s *= 2
    return a, u


def _rnn_kernel(rg_ref, cw_ref, cb_ref, wd_ref, ba_ref, bx_ref, lam_ref, y_ref,
                xbuf_ref, hc_ref):
    tb = y_ref.shape[0]
    d = y_ref.shape[1]
    halo = SUBLANES

    @pl.when(pl.program_id(1) == 0)
    def _():
        xbuf_ref[0:halo, :] = jnp.zeros((halo, d), F32)
        hc_ref[...] = jnp.zeros_like(hc_ref)

    x = rg_ref[:, 0:d].astype(F32)
    xbuf_ref[halo:halo + tb, :] = x
    xc = cb_ref[...] + cw_ref[CONV_WIDTH - 1:CONV_WIDTH, :] * x
    for j in range(CONV_WIDTH - 1):
        off = halo - (CONV_WIDTH - 1) + j
        xc = xc + cw_ref[j:j + 1, :] * xbuf_ref[off:off + tb, :]
    xbuf_ref[0:halo, :] = x[tb - halo:, :]

    xcb = xc.astype(BF16)
    n_slab = wd_ref.shape[0]
    sw = d // n_slab
    pre_a, pre_i = [], []
    for j in range(n_slab):
        ga = _dot(xcb[:, j * sw:(j + 1) * sw], wd_ref[j])
        pre_a.append(ga[:, :sw])
        pre_i.append(ga[:, sw:])
    r = jax.nn.sigmoid(jnp.concatenate(pre_a, axis=1) + ba_ref[...])
    ig = jax.nn.sigmoid(jnp.concatenate(pre_i, axis=1) + bx_ref[...])
    log_a = (-LRU_C) * r * jax.nn.softplus(-lam_ref[...])
    a = jnp.exp(log_a)
    mult = jnp.sqrt(1.0 - a * a)
    u = mult * (ig * xc)

    a_cum, h = _linear_scan(a, u)
    h = h + a_cum * hc_ref[...]
    hc_ref[...] = h[tb - 1:tb, :]

    gate = rg_ref[:, d:2 * d].astype(F32)
    y_ref[...] = (h * jax.nn.gelu(gate)).astype(BF16)


def _rnn_call(rg, params, B, T, tb):
    cw, cb, wd, ba, bx, lam = params
    N = rg.shape[0]
    d = rg.shape[1] // 2
    nt = T // tb
    return pl.pallas_call(
        _rnn_kernel,
        grid=(B, nt),
        in_specs=[pl.BlockSpec((tb, 2 * d), lambda b, i: (b * nt + i, 0))]
        + [_const_spec(p.shape) for p in params],
        out_specs=pl.BlockSpec((tb, d), lambda b, i: (b * nt + i, 0)),
        out_shape=jax.ShapeDtypeStruct((N, d), BF16),
        scratch_shapes=[pltpu.VMEM((tb + SUBLANES, d), F32), pltpu.VMEM((1, d), F32)],
        compiler_params=pltpu.CompilerParams(
            dimension_semantics=("arbitrary", "arbitrary"),
            vmem_limit_bytes=VMEM_LIMIT_BYTES),
        name="rglru",
    )(rg, *params)


def _fold_rows(x, op):
    return op(x.reshape(x.shape[0] // SUBLANES, SUBLANES, x.shape[1]), axis=0)


def _attn_kernel(q_ref, iq_ref, iwt_ref, klo_ref, khi_ref, vt_ref, iklo_ref, ikhi_ref, o_ref,
                 sc_ref, bias_ref, s_ref, p_ref, alpha_ref, m_ref, l_ref, acc_ref, *, top_k):
    tq = q_ref.shape[0]
    tk = tq
    qi = pl.program_id(1)
    n_pairs = q_ref.shape[1] // LANES
    group = N_HEADS // N_KV_HEADS
    n_sub = tk // CHUNK

    def idx_block(kb, diagonal):
        ks = pl.multiple_of(kb * tk, tk)
        ik_lo = iklo_ref[0, pl.ds(ks, tk), :]
        ik_hi = ikhi_ref[0, pl.ds(ks, tk), :]
        score = jnp.zeros((tk, tq), F32)
        for p in range(iq_ref.shape[1] // LANES):
            iqp = iq_ref[:, p * LANES:(p + 1) * LANES]
            score = score + jax.nn.relu(_dot_nt(ik_lo, iqp)) * iwt_ref[0, 2 * p:2 * p + 1, :]
            score = score + jax.nn.relu(_dot_nt(ik_hi, iqp)) * iwt_ref[0, 2 * p + 1:2 * p + 2, :]
        if diagonal:
            kc = lax.broadcasted_iota(jnp.int32, (tk, tq), 0) // CHUNK
            qc = lax.broadcasted_iota(jnp.int32, (tk, tq), 1) // CHUNK
            visible = kc <= qc
            sc_ref[pl.ds(ks, tk), :] = jnp.where(visible, score, -jnp.inf)
            return (_fold_rows(jnp.where(visible, score, jnp.inf), jnp.min),
                    _fold_rows(jnp.where(visible, score, -jnp.inf), jnp.max))
        sc_ref[pl.ds(ks, tk), :] = score
        return _fold_rows(score, jnp.min), _fold_rows(score, jnp.max)

    def idx_body(kb, carry):
        mn, mx = idx_block(kb, False)
        return jnp.minimum(carry[0], mn), jnp.maximum(carry[1], mx)

    mn, mx = lax.fori_loop(0, qi, idx_body, (jnp.full((SUBLANES, tq), jnp.inf, F32),
                                             jnp.full((SUBLANES, tq), -jnp.inf, F32)))
    mn_d, mx_d = idx_block(qi, True)
    lo0 = jnp.min(jnp.minimum(mn, mn_d), axis=0, keepdims=True)
    hi0 = jnp.max(jnp.maximum(mx, mx_d), axis=0, keepdims=True)
    hi0 = hi0 + (jnp.abs(hi0) * 2.0 ** -20 + 1e-30)

    def count(pred, trips):
        def body(kb, cnt):
            ks = pl.multiple_of(kb * tk, tk)
            pos = ks + lax.broadcasted_iota(jnp.int32, (tk, tq), 0)
            hit = jnp.where(pred(sc_ref[pl.ds(ks, tk), :], pos), 1, 0)
            return cnt + _fold_rows(hit, jnp.sum)
        cnt = lax.fori_loop(0, trips, body, jnp.zeros((SUBLANES, tq), jnp.int32))
        return jnp.sum(cnt, axis=0, keepdims=True)

    def any_row(flag):
        return (jnp.max(jnp.where(flag, 1.0, 0.0)) > 0.0).astype(jnp.int32)

    n_kb = qi + 1
    n_visible = qi * tq + (lax.broadcasted_iota(jnp.int32, (1, tq), 1) // CHUNK + 1) * CHUNK

    def bisect_round(state):
        lo, hi, c_lo, _, rounds = state
        for _ in range(BISECT_STEPS_PER_ROUND):
            mid = lo + 0.5 * (hi - lo)
            c = count(lambda s, pos: s >= mid, n_kb)
            ok = c >= top_k
            lo, hi, c_lo = jnp.where(ok, mid, lo), jnp.where(ok, hi, mid), jnp.where(ok, c, c_lo)
        return lo, hi, c_lo, any_row(c_lo > top_k), rounds + 1

    lo, hi, c_lo, _, _ = lax.while_loop(
        lambda st: (st[3] > 0) & (st[4] < BISECT_MAX_ROUNDS), bisect_round,
        (lo0, hi0, n_visible, any_row(n_visible > top_k), jnp.int32(0)))

    def snap(state):
        lo, hi, c_lo, todo, _ = state
        def below_hi(kb, best):
            ks = pl.multiple_of(kb * tk, tk)
            s = sc_ref[pl.ds(ks, tk), :]
            return jnp.maximum(best, _fold_rows(jnp.where(s < hi, s, -jnp.inf), jnp.max))
        v = jnp.max(lax.fori_loop(0, n_kb, below_hi, jnp.full((SUBLANES, tq), -jnp.inf, F32)),
                    axis=0, keepdims=True)
        c_v = count(lambda s, pos: s >= v, n_kb)
        found = (todo > 0) & (c_v >= top_k)
        lo, c_lo = jnp.where(found, v, lo), jnp.where(found, c_v, c_lo)
        hi = jnp.where((todo > 0) & (c_v < top_k), v, hi)
        todo = jnp.where(found, 0, todo)
        return lo, hi, c_lo, todo, any_row(todo > 0)

    unsettled = jnp.where(c_lo > top_k, 1, 0)
    thr, _, c_thr, _, _ = lax.while_loop(lambda st: st[4] > 0, snap,
                                         (lo, hi, c_lo, unsettled, any_row(unsettled > 0)))

    tied = c_thr > top_k
    tie_trips = any_row(tied)
    need = top_k - count(lambda s, pos: s > thr, tie_trips * n_kb)
    n_bits = (sc_ref.shape[0] - 1).bit_length()

    def last_bit(i, last):
        cand = last + jnp.left_shift(jnp.int32(1), n_bits - 1 - i)
        before = count(lambda s, pos: (s == thr) & (pos < cand), n_kb)
        return jnp.where(before < need, cand, last)

    last = lax.fori_loop(0, tie_trips * n_bits, last_bit, jnp.zeros((1, tq), jnp.int32))
    last = jnp.where(tied, last, sc_ref.shape[0])

    m_ref[...] = jnp.full(m_ref.shape, MASKED, F32)
    l_ref[...] = jnp.zeros(l_ref.shape, F32)
    acc_ref[...] = jnp.zeros(acc_ref.shape, F32)

    def att_body(kb, carry):
        ks = pl.multiple_of(kb * tk, tk)
        sc = sc_ref[pl.ds(ks, tk), :]
        pos = ks + lax.broadcasted_iota(jnp.int32, (tk, tq), 0)
        bias_ref[...] = jnp.where(sc > thr, 0.0, jnp.where((sc == thr) & (pos <= last), 0.0, MASKED))
        for j in range(n_pairs):
            qp = q_ref[:, j * LANES:(j + 1) * LANES]
            g = (2 * j) // group
            for half in range(2):
                kx = (klo_ref if half == 0 else khi_ref)[0, g, pl.ds(ks, tk), :]
                s_ref[2 * j + half] = _dot_nt(kx, qp) + bias_ref[...]
        for h in range(N_HEADS):
            m_old = m_ref[h:h + 1, :]
            mb = _fold_rows(s_ref[h, 0:CHUNK, :], jnp.max)
            for c in range(1, n_sub):
                mb = jnp.maximum(mb, _fold_rows(s_ref[h, c * CHUNK:(c + 1) * CHUNK, :], jnp.max))
            m_new = jnp.maximum(m_old, jnp.max(mb, axis=0, keepdims=True))
            alpha = jnp.exp2(m_old - m_new)
            lsum = jnp.zeros((SUBLANES, tq), F32)
            for c in range(n_sub):
                rows = slice(c * CHUNK, (c + 1) * CHUNK)
                p = jnp.exp2(s_ref[h, rows, :] - m_new)
                lsum = lsum + _fold_rows(p, jnp.sum)
                p_ref[h, rows, :] = p.astype(BF16)
            l_ref[h:h + 1, :] = alpha * l_ref[h:h + 1, :] + jnp.sum(lsum, axis=0, keepdims=True)
            m_ref[h:h + 1, :] = m_new
            alpha_ref[h:h + 1, :] = alpha
        for j in range(n_pairs):
            g = (2 * j) // group
            vt = vt_ref[0, g * HEAD_DIM:(g + 1) * HEAD_DIM, pl.ds(ks, tk)]
            for half in range(2):
                h = 2 * j + half
                rows = slice(half * HEAD_DIM, (half + 1) * HEAD_DIM)
                acc_ref[j, rows, :] = (alpha_ref[h:h + 1, :] * acc_ref[j, rows, :]
                                       + _dot(vt, p_ref[h]))
        return carry

    lax.fori_loop(0, qi + 1, att_body, 0)

    for j in range(n_pairs):
        inv = jnp.concatenate(
            [jnp.broadcast_to(1.0 / l_ref[2 * j + half:2 * j + half + 1, :], (HEAD_DIM, tq))
             for half in range(2)], axis=0)
        o_ref[:, j * LANES:(j + 1) * LANES] = (acc_ref[j] * inv).T.astype(BF16)


def _attn_call(q, iq, iwt, klo, khi, vt, iklo, ikhi, B, T, tq):
    N, d_q = q.shape
    nq = T // tq
    top_k = min(TOPK_MAX, T // 4)
    row = lambda w: pl.BlockSpec((tq, w), lambda b, i: (b * nq + i, 0))
    per_batch = lambda shape: pl.BlockSpec((1,) + shape[1:], lambda b, i: (b,) + (0,) * (len(shape) - 1))
    return pl.pallas_call(
        functools.partial(_attn_kernel, top_k=top_k),
        grid=(B, nq),
        in_specs=[row(d_q), row(iq.shape[1]),
                  pl.BlockSpec((1, IDX_HEADS, tq), lambda b, i: (b, 0, i)),
                  per_batch(klo.shape), per_batch(khi.shape), per_batch(vt.shape),
                  per_batch(iklo.shape), per_batch(ikhi.shape)],
        out_specs=row(d_q),
        out_shape=jax.ShapeDtypeStruct((N, d_q), BF16),
        scratch_shapes=[pltpu.VMEM((T, tq), F32),
                        pltpu.VMEM((tq, tq), F32),
                        pltpu.VMEM((N_HEADS, tq, tq), F32),
                        pltpu.VMEM((N_HEADS, tq, tq), BF16),
                        pltpu.VMEM((N_HEADS, tq), F32),
                        pltpu.VMEM((N_HEADS, tq), F32),
                        pltpu.VMEM((N_HEADS, tq), F32),
                        pltpu.VMEM((N_HEADS // 2, LANES, tq), F32)],
        compiler_params=pltpu.CompilerParams(
            dimension_semantics=("arbitrary", "arbitrary"),
            vmem_limit_bytes=VMEM_LIMIT_BYTES),
        name="dsa",
    )(q, iq, iwt, klo, khi, vt, iklo, ikhi)


def _mlp_kernel(x_ref, gates_ref, yr_ref, ya_ref, wor_ref, woa_ref, wout_ref, n2_ref,
                wfi_ref, wfo_ref, o_ref, *, ff_chunk):
    d = x_ref.shape[1]
    merged = (gates_ref[:, 0:d].astype(F32) * _dot(yr_ref[...], wor_ref[...])
              + gates_ref[:, d:2 * d].astype(F32) * _dot(ya_ref[...], woa_ref[...]))
    x1 = x_ref[...] + _dot(merged.astype(BF16), wout_ref[...])
    ms = jnp.mean(x1 * x1, axis=-1, keepdims=True)
    h2 = (x1 * lax.rsqrt(ms + EPS) * n2_ref[...]).astype(BF16)
    acc = x1
    for c in range(wfi_ref.shape[1] // ff_chunk):
        sl = slice(c * ff_chunk, (c + 1) * ff_chunk)
        hid = jnp.square(jax.nn.relu(_dot(h2, wfi_ref[:, sl])))
        acc = acc + _dot(hid.astype(BF16), wfo_ref[sl, :])
    o_ref[...] = acc


def _mlp_call(x2d, gates, y_rnn, y_attn, params, tm):
    N, D = x2d.shape
    row = lambda w: pl.BlockSpec((tm, w), lambda i: (i, 0))
    return pl.pallas_call(
        functools.partial(_mlp_kernel, ff_chunk=1024),
        grid=(N // tm,),
        in_specs=[row(D), row(gates.shape[1]), row(D), row(D)]
        + [_const_spec(p.shape) for p in params],
        out_specs=row(D),
        out_shape=jax.ShapeDtypeStruct((N, D), F32),
        compiler_params=pltpu.CompilerParams(
            dimension_semantics=("arbitrary",),
            vmem_limit_bytes=VMEM_LIMIT_BYTES),
        name="merge_mlp",
    )(x2d, gates, y_rnn, y_attn, *params)


def _block_diag_slabs(wa, wx, slab):
    nb, bd, _ = wa.shape
    per = slab // bd

    def diag(w):
        w = w.reshape(nb // per, per, bd, bd)
        eye = jnp.eye(per, dtype=w.dtype)
        return jnp.einsum('spde,pq->spdqe', w, eye).reshape(nb // per, slab, slab)

    return jnp.concatenate([diag(wa), diag(wx)], axis=-1)


def _layer(x, norm1_w, w_in, conv_w, conv_b, rg_wa, rg_ba, rg_wx, rg_bx, rg_lambda,
           q_norm_w, k_norm_w, idx_k_norm_w, w_o_rnn, w_o_attn, gate_b, w_out,
           norm2_w, w_ff_in, w_ff_out):
    B, T, D = x.shape
    d_rnn = conv_w.shape[1]
    d_attn = N_HEADS * HEAD_DIM
    d_kv = N_KV_HEADS * HEAD_DIM
    d_iq = IDX_HEADS * IDX_DIM
    sizes = (d_rnn, d_rnn, d_attn, d_kv, d_kv, d_iq, IDX_DIM, IDX_HEADS, 2 * D)
    offs = [0]
    for s in sizes:
        offs.append(offs[-1] + s)
    col = lambda i: w_in[:, offs[i]:offs[i + 1]]
    row = lambda v: v.reshape(1, -1).astype(F32)

    proj_params = (
        row(norm1_w),
        jnp.concatenate([col(0), col(1)], axis=1).astype(BF16),
        col(8).astype(BF16),
        col(2).astype(BF16),
        col(3).astype(BF16),
        col(5).astype(BF16),
        jnp.pad(col(6), ((0, 0), (0, LANES - IDX_DIM))).astype(BF16),
        col(4).T.astype(BF16),
        col(7).T.astype(BF16),
        row(gate_b),
        row(jnp.tile(q_norm_w, LANES // HEAD_DIM)) * (HEAD_DIM ** -0.5 * LOG2_E),
        row(jnp.tile(k_norm_w, LANES // HEAD_DIM)),
        row(jnp.pad(idx_k_norm_w, (0, LANES - IDX_DIM))),
    )
    x2d = x.reshape(B * T, D)
    rg, gates, q, klo, khi, vt, iq, iklo, ikhi, iwt = _proj_call(x2d, proj_params, B, T, tm=512)

    rnn_params = (conv_w.astype(F32), row(conv_b),
                  _block_diag_slabs(rg_wa, rg_wx, 2 * LANES).astype(BF16),
                  row(rg_ba), row(rg_bx), row(rg_lambda))
    y_rnn = _rnn_call(rg, rnn_params, B, T, tb=256)

    y_attn = _attn_call(q, iq, iwt, klo, khi, vt, iklo, ikhi, B, T, tq=256)

    mlp_params = (w_o_rnn.astype(BF16), w_o_attn.astype(BF16), w_out.astype(BF16), row(norm2_w),
                  w_ff_in.astype(BF16), w_ff_out.astype(BF16))
    out = _mlp_call(x2d, gates, y_rnn, y_attn, mlp_params, tm=256)
    return out.reshape(B, T, D)


def kernel(x, norm1_w, w_in, conv_w, conv_b, rg_wa, rg_ba, rg_wx, rg_bx, rg_lambda, q_norm_w,
           k_norm_w, idx_k_norm_w, w_o_rnn, w_o_attn, gate_b, w_out, norm2_w, w_ff_in, w_ff_out):
    for l in range(norm1_w.shape[0]):
        x = _layer(x, norm1_w[l], w_in[l], conv_w[l], conv_b[l], rg_wa[l], rg_ba[l], rg_wx[l],
                   rg_bx[l], rg_lambda[l], q_norm_w[l], k_norm_w[l], idx_k_norm_w[l], w_o_rnn[l],
                   w_o_attn[l], gate_b[l], w_out[l], norm2_w[l], w_ff_in[l], w_ff_out[l])
    return x
```

```python
import functools

import jax
import jax.numpy as jnp
from jax import lax
from jax.experimental import pallas as pl
from jax.experimental.pallas import tpu as pltpu

CHUNK = 64
RNN_BLOCKS = 16
CONV_WIDTH = 4
LRU_C = 8.0
N_HEADS = 16
N_KV_HEADS = 4
HEAD_DIM = 64
IDX_HEADS = 8
IDX_DIM = 64
TOPK_MAX = 256
EPS = 1e-6

LANES = 128
SUBLANES = 8
VMEM_LIMIT_BYTES = 56 * 1024 * 1024

MASKED = -1e30
LOG2_E = 1.4426950408889634
SHIFT_MARGIN = 1.02
SOFTMAX_SHIFT_LIMIT = 50.0
BF16_SUBLANES = 16
VT_ROWS = HEAD_DIM + BF16_SUBLANES
BISECT_STEPS_PER_ROUND = 4
BISECT_MAX_ROUNDS = 8

BF16 = jnp.bfloat16
F32 = jnp.float32


def _dot(a, b):
    return jnp.dot(a, b, preferred_element_type=F32)


def _dot_nt(a, b):
    return lax.dot_general(a, b, (((1,), (1,)), ((), ())), preferred_element_type=F32)


def _segment_rms_scale(x, seg):
    x2 = x * x
    lane = lax.broadcasted_iota(jnp.int32, x.shape, 1)
    lo = lane < seg
    s_lo = jnp.sum(jnp.where(lo, x2, 0.0), axis=-1, keepdims=True)
    s_hi = jnp.sum(x2, axis=-1, keepdims=True) - s_lo
    r_lo = lax.rsqrt(s_lo * (1.0 / seg) + EPS)
    r_hi = lax.rsqrt(s_hi * (1.0 / seg) + EPS)
    return jnp.where(lo, r_lo, r_hi)


def _proj_kernel(x_ref, n1_ref, wrg_ref, wg_ref, wq_ref, wk_ref, wiq_ref, wik_ref, wvt_ref,
                 wiwt_ref, gb_ref, qnw_ref, knw_ref, iknw_ref,
                 rg_ref, gates_ref, q_ref, klo_ref, khi_ref, vt_ref, iq_ref, iklo_ref, ikhi_ref,
                 iwt_ref, *, idx_scale):
    x = x_ref[...]
    ms = jnp.mean(x * x, axis=-1, keepdims=True)
    h = (x * lax.rsqrt(ms + EPS) * n1_ref[...]).astype(BF16)

    rg_ref[...] = _dot(h, wrg_ref[...]).astype(BF16)
    gates_ref[...] = jax.nn.sigmoid(_dot(h, wg_ref[...]) + gb_ref[...]).astype(BF16)
    iq_ref[...] = _dot(h, wiq_ref[...]).astype(BF16)

    q = _dot(h, wq_ref[...])
    for j in range(q.shape[1] // LANES):
        sl = slice(j * LANES, (j + 1) * LANES)
        qj = q[:, sl]
        q_ref[:, sl] = (qj * _segment_rms_scale(qj, HEAD_DIM) * qnw_ref[...]).astype(BF16)

    k = _dot(h, wk_ref[...])
    lane = lax.broadcasted_iota(jnp.int32, (k.shape[0], LANES), 1)
    lo = lane < HEAD_DIM
    for j in range(k.shape[1] // LANES):
        kj = k[:, j * LANES:(j + 1) * LANES]
        kj = kj * _segment_rms_scale(kj, HEAD_DIM) * knw_ref[...]
        ks = pltpu.roll(kj, HEAD_DIM, axis=1)
        klo_ref[0, 2 * j] = jnp.where(lo, kj, 0.0).astype(BF16)
        khi_ref[0, 2 * j] = jnp.where(lo, 0.0, ks).astype(BF16)
        klo_ref[0, 2 * j + 1] = jnp.where(lo, ks, 0.0).astype(BF16)
        khi_ref[0, 2 * j + 1] = jnp.where(lo, 0.0, kj).astype(BF16)

    ik = _dot(h, wik_ref[...])
    ik_ms = jnp.sum(ik * ik, axis=-1, keepdims=True) * (1.0 / IDX_DIM)
    ik = ik * lax.rsqrt(ik_ms + EPS) * iknw_ref[...]
    iklo_ref[0] = ik.astype(BF16)
    ikhi_ref[0] = pltpu.roll(ik, IDX_DIM, axis=1).astype(BF16)

    vt = _dot_nt(wvt_ref[...], h).astype(BF16)
    ones = jnp.ones((VT_ROWS - HEAD_DIM, vt.shape[1]), BF16)
    for g in range(N_KV_HEADS):
        vt_ref[0, g] = jnp.concatenate([vt[g * HEAD_DIM:(g + 1) * HEAD_DIM, :], ones], axis=0)
    iwt_ref[0] = _dot_nt(wiwt_ref[...], h) * idx_scale


def _const_spec(shape):
    nd = len(shape)
    return pl.BlockSpec(shape, lambda *_: (0,) * nd)


def _proj_call(x2d, params, B, T, tm):
    (n1, wrg, wg, wq, wk, wiq, wik, wvt, wiwt, gb, qnw, knw, iknw) = params
    N, D = x2d.shape
    nt = T // tm
    d_rg, d_g, d_q, d_k, d_iq = wrg.shape[1], wg.shape[1], wq.shape[1], wk.shape[1], wiq.shape[1]
    idx_scale = (IDX_DIM ** -0.5) * (IDX_HEADS ** -0.5)

    row = lambda w: pl.BlockSpec((tm, w), lambda b, i: (b * nt + i, 0))
    in_specs = [row(D)] + [_const_spec(p.shape) for p in params]
    out_shape = (
        jax.ShapeDtypeStruct((N, d_rg), BF16),
        jax.ShapeDtypeStruct((N, d_g), BF16),
        jax.ShapeDtypeStruct((N, d_q), BF16),
        jax.ShapeDtypeStruct((B, N_KV_HEADS, T, LANES), BF16),
        jax.ShapeDtypeStruct((B, N_KV_HEADS, T, LANES), BF16),
        jax.ShapeDtypeStruct((B, N_KV_HEADS, VT_ROWS, T), BF16),
        jax.ShapeDtypeStruct((N, d_iq), BF16),
        jax.ShapeDtypeStruct((B, T, LANES), BF16),
        jax.ShapeDtypeStruct((B, T, LANES), BF16),
        jax.ShapeDtypeStruct((B, IDX_HEADS, T), F32),
    )
    kv_spec = pl.BlockSpec((1, N_KV_HEADS, tm, LANES), lambda b, i: (b, 0, i, 0))
    ik_spec = pl.BlockSpec((1, tm, LANES), lambda b, i: (b, i, 0))
    out_specs = (
        row(d_rg), row(d_g), row(d_q), kv_spec, kv_spec,
        pl.BlockSpec((1, N_KV_HEADS, VT_ROWS, tm), lambda b, i: (b, 0, 0, i)),
        row(d_iq), ik_spec, ik_spec,
        pl.BlockSpec((1, IDX_HEADS, tm), lambda b, i: (b, 0, i)),
    )
    return pl.pallas_call(
        functools.partial(_proj_kernel, idx_scale=idx_scale),
        grid=(B, nt),
        in_specs=in_specs,
        out_specs=out_specs,
        out_shape=out_shape,
        compiler_params=pltpu.CompilerParams(
            dimension_semantics=("arbitrary", "arbitrary"),
            vmem_limit_bytes=VMEM_LIMIT_BYTES),
        name="proj",
    )(x2d, *params)


def _shift_rows(x, s, fill):
    rolled = pltpu.roll(x, s, axis=0)
    rows = lax.broadcasted_iota(jnp.int32, x.shape, 0)
    return jnp.where(rows >= s, rolled, fill)


def _linear_scan(a, u):
    tb = a.shape[0]
    s = 1
    while s < tb:
        if s < SUBLANES:
            a_s = _shift_rows(a, s, 1.0)
            u_s = _shift_rows(u, s, 0.0)
            u = a * u_s + u
            a = a * a_s
        else:
            u = jnp.concatenate([u[:s], a[s:] * u[:-s] + u[s:]], axis=0)
            a = jnp.concatenate([a[:s], a[s:] * a[:-s]], axis=0)
        s *= 2
    return a, u


def _rnn_kernel(rg_ref, cw_ref, cb_ref, wd_ref, ba_ref, bx_ref, lam_ref, y_ref,
                xbuf_ref, hc_ref):
    tb = y_ref.shape[0]
    d = y_ref.shape[1]
    halo = SUBLANES

    @pl.when(pl.program_id(1) == 0)
    def _():
        xbuf_ref[0:halo, :] = jnp.zeros((halo, d), F32)
        hc_ref[...] = jnp.zeros_like(hc_ref)

    x = rg_ref[:, 0:d].astype(F32)
    xbuf_ref[halo:halo + tb, :] = x
    xc = cb_ref[...] + cw_ref[CONV_WIDTH - 1:CONV_WIDTH, :] * x
    for j in range(CONV_WIDTH - 1):
        off = halo - (CONV_WIDTH - 1) + j
        xc = xc + cw_ref[j:j + 1, :] * xbuf_ref[off:off + tb, :]
    xbuf_ref[0:halo, :] = x[tb - halo:, :]

    xcb = xc.astype(BF16)
    n_slab = wd_ref.shape[0]
    sw = d // n_slab
    pre_a, pre_i = [], []
    for j in range(n_slab):
        ga = _dot(xcb[:, j * sw:(j + 1) * sw], wd_ref[j])
        pre_a.append(ga[:, :sw])
        pre_i.append(ga[:, sw:])
    r = jax.nn.sigmoid(jnp.concatenate(pre_a, axis=1) + ba_ref[...])
    ig = jax.nn.sigmoid(jnp.concatenate(pre_i, axis=1) + bx_ref[...])
    log_a = (-LRU_C) * r * jax.nn.softplus(-lam_ref[...])
    a = jnp.exp(log_a)
    mult = jnp.sqrt(1.0 - a * a)
    u = mult * (ig * xc)

    a_cum, h = _linear_scan(a, u)
    h = h + a_cum * hc_ref[...]
    hc_ref[...] = h[tb - 1:tb, :]

    gate = rg_ref[:, d:2 * d].astype(F32)
    y_ref[...] = (h * jax.nn.gelu(gate)).astype(BF16)


def _rnn_call(rg, params, B, T, tb):
    cw, cb, wd, ba, bx, lam = params
    N = rg.shape[0]
    d = rg.shape[1] // 2
    nt = T // tb
    return pl.pallas_call(
        _rnn_kernel,
        grid=(B, nt),
        in_specs=[pl.BlockSpec((tb, 2 * d), lambda b, i: (b * nt + i, 0))]
        + [_const_spec(p.shape) for p in params],
        out_specs=pl.BlockSpec((tb, d), lambda b, i: (b * nt + i, 0)),
        out_shape=jax.ShapeDtypeStruct((N, d), BF16),
        scratch_shapes=[pltpu.VMEM((tb + SUBLANES, d), F32), pltpu.VMEM((1, d), F32)],
        compiler_params=pltpu.CompilerParams(
            dimension_semantics=("arbitrary", "arbitrary"),
            vmem_limit_bytes=VMEM_LIMIT_BYTES),
        name="rglru",
    )(rg, *params)


def _fold_rows(x, op):
    return op(x.reshape(x.shape[0] // SUBLANES, SUBLANES, x.shape[1]), axis=0)


def _attn_kernel(shift_ref, q_ref, iq_ref, iwt_ref, klo_ref, khi_ref, vt_ref, iklo_ref, ikhi_ref,
                 o_ref, sc_ref, bias_ref, s_ref, p_ref, accv_ref, m_ref, l_ref, acc_ref, *, top_k):
    tq = q_ref.shape[0]
    tk = tq
    qi = pl.program_id(1)
    n_pairs = q_ref.shape[1] // LANES
    group = N_HEADS // N_KV_HEADS
    n_sub = tk // CHUNK

    def idx_block(kb, diagonal):
        ks = pl.multiple_of(kb * tk, tk)
        ik_lo = iklo_ref[0, pl.ds(ks, tk), :]
        ik_hi = ikhi_ref[0, pl.ds(ks, tk), :]
        score = jnp.zeros((tk, tq), F32)
        for p in range(iq_ref.shape[1] // LANES):
            iqp = iq_ref[:, p * LANES:(p + 1) * LANES]
            score = score + jax.nn.relu(_dot_nt(ik_lo, iqp)) * iwt_ref[0, 2 * p:2 * p + 1, :]
            score = score + jax.nn.relu(_dot_nt(ik_hi, iqp)) * iwt_ref[0, 2 * p + 1:2 * p + 2, :]
        if diagonal:
            kc = lax.broadcasted_iota(jnp.int32, (tk, tq), 0) // CHUNK
            qc = lax.broadcasted_iota(jnp.int32, (tk, tq), 1) // CHUNK
            visible = kc <= qc
            sc_ref[pl.ds(ks, tk), :] = jnp.where(visible, score, -jnp.inf)
            return (_fold_rows(jnp.where(visible, score, jnp.inf), jnp.min),
                    _fold_rows(jnp.where(visible, score, -jnp.inf), jnp.max))
        sc_ref[pl.ds(ks, tk), :] = score
        return _fold_rows(score, jnp.min), _fold_rows(score, jnp.max)

    def idx_body(kb, carry):
        mn, mx = idx_block(kb, False)
        return jnp.minimum(carry[0], mn), jnp.maximum(carry[1], mx)

    mn, mx = lax.fori_loop(0, qi, idx_body, (jnp.full((SUBLANES, tq), jnp.inf, F32),
                                             jnp.full((SUBLANES, tq), -jnp.inf, F32)))
    mn_d, mx_d = idx_block(qi, True)
    lo0 = jnp.min(jnp.minimum(mn, mn_d), axis=0, keepdims=True)
    hi0 = jnp.max(jnp.maximum(mx, mx_d), axis=0, keepdims=True)
    hi0 = hi0 + (jnp.abs(hi0) * 2.0 ** -20 + 1e-30)

    def count(pred, trips):
        def body(kb, cnt):
            ks = pl.multiple_of(kb * tk, tk)
            pos = ks + lax.broadcasted_iota(jnp.int32, (tk, tq), 0)
            hit = jnp.where(pred(sc_ref[pl.ds(ks, tk), :], pos), 1, 0)
            return cnt + _fold_rows(hit, jnp.sum)
        cnt = lax.fori_loop(0, trips, body, jnp.zeros((SUBLANES, tq), jnp.int32))
        return jnp.sum(cnt, axis=0, keepdims=True)

    def any_row(flag):
        return (jnp.max(jnp.where(flag, 1.0, 0.0)) > 0.0).astype(jnp.int32)

    n_kb = qi + 1
    n_visible = qi * tq + (lax.broadcasted_iota(jnp.int32, (1, tq), 1) // CHUNK + 1) * CHUNK

    def bisect_round(state):
        lo, hi, c_lo, _, rounds = state
        for _ in range(BISECT_STEPS_PER_ROUND):
            mid = lo + 0.5 * (hi - lo)
            c = count(lambda s, pos: s >= mid, n_kb)
            ok = c >= top_k
            lo, hi, c_lo = jnp.where(ok, mid, lo), jnp.where(ok, hi, mid), jnp.where(ok, c, c_lo)
        return lo, hi, c_lo, any_row(c_lo > top_k), rounds + 1

    lo, hi, c_lo, _, _ = lax.while_loop(
        lambda st: (st[3] > 0) & (st[4] < BISECT_MAX_ROUNDS), bisect_round,
        (lo0, hi0, n_visible, any_row(n_visible > top_k), jnp.int32(0)))

    def snap(state):
        lo, hi, c_lo, todo, _ = state
        def below_hi(kb, best):
            ks = pl.multiple_of(kb * tk, tk)
            s = sc_ref[pl.ds(ks, tk), :]
            return jnp.maximum(best, _fold_rows(jnp.where(s < hi, s, -jnp.inf), jnp.max))
        v = jnp.max(lax.fori_loop(0, n_kb, below_hi, jnp.full((SUBLANES, tq), -jnp.inf, F32)),
                    axis=0, keepdims=True)
        c_v = count(lambda s, pos: s >= v, n_kb)
        found = (todo > 0) & (c_v >= top_k)
        lo, c_lo = jnp.where(found, v, lo), jnp.where(found, c_v, c_lo)
        hi = jnp.where((todo > 0) & (c_v < top_k), v, hi)
        todo = jnp.where(found, 0, todo)
        return lo, hi, c_lo, todo, any_row(todo > 0)

    unsettled = jnp.where(c_lo > top_k, 1, 0)
    thr, _, c_thr, _, _ = lax.while_loop(lambda st: st[4] > 0, snap,
                                         (lo, hi, c_lo, unsettled, any_row(unsettled > 0)))

    tied = c_thr > top_k
    tie_trips = any_row(tied)
    need = top_k - count(lambda s, pos: s > thr, tie_trips * n_kb)
    n_bits = (sc_ref.shape[0] - 1).bit_length()

    def last_bit(i, last):
        cand = last + jnp.left_shift(jnp.int32(1), n_bits - 1 - i)
        before = count(lambda s, pos: (s == thr) & (pos < cand), n_kb)
        return jnp.where(before < need, cand, last)

    last = lax.fori_loop(0, tie_trips * n_bits, last_bit, jnp.zeros((1, tq), jnp.int32))
    last = jnp.where(tied, last, sc_ref.shape[0])

    def selection_bias(ks, selected_value):
        sc = sc_ref[pl.ds(ks, tk), :]
        pos = ks + lax.broadcasted_iota(jnp.int32, (tk, tq), 0)
        tie_value = jnp.where((sc == thr) & (pos <= last), selected_value, MASKED)
        return jnp.where(sc > thr, selected_value, tie_value)

    def all_head_scores(ks):
        for j in range(n_pairs):
            qp = q_ref[:, j * LANES:(j + 1) * LANES]
            g = (2 * j) // group
            for half in range(2):
                kx = (klo_ref if half == 0 else khi_ref)[0, g, pl.ds(ks, tk), :]
                s_ref[2 * j + half] = _dot_nt(kx, qp) + bias_ref[...]

    def write_output(head_out):
        for j in range(n_pairs):
            pair = jnp.concatenate([head_out(2 * j), head_out(2 * j + 1)], axis=0)
            o_ref[:, j * LANES:(j + 1) * LANES] = pair.T.astype(BF16)

    shift = shift_ref[0]

    @pl.when(shift <= SOFTMAX_SHIFT_LIMIT)
    def _fixed_shift():
        accv_ref[...] = jnp.zeros(accv_ref.shape, F32)

        def body(kb, carry):
            ks = pl.multiple_of(kb * tk, tk)
            bias_ref[...] = selection_bias(ks, -shift)
            all_head_scores(ks)
            for h in range(N_HEADS):
                for c in range(n_sub):
                    rows = slice(c * CHUNK, (c + 1) * CHUNK)
                    p_ref[h, rows, :] = jnp.exp2(s_ref[h, rows, :]).astype(BF16)
            for h in range(N_HEADS):
                accv_ref[h] = accv_ref[h] + _dot(vt_ref[0, h // group, :, pl.ds(ks, tk)], p_ref[h])
            return carry

        lax.fori_loop(0, n_kb, body, 0)
        write_output(lambda h: accv_ref[h, 0:HEAD_DIM, :] * (1.0 / accv_ref[h, HEAD_DIM:HEAD_DIM + 1, :]))

    @pl.when(shift > SOFTMAX_SHIFT_LIMIT)
    def _running_max():
        m_ref[...] = jnp.full(m_ref.shape, MASKED, F32)
        l_ref[...] = jnp.zeros(l_ref.shape, F32)
        acc_ref[...] = jnp.zeros(acc_ref.shape, F32)

        def body(kb, carry):
            ks = pl.multiple_of(kb * tk, tk)
            bias_ref[...] = selection_bias(ks, 0.0)
            all_head_scores(ks)
            for h in range(N_HEADS):
                m_old = m_ref[h:h + 1, :]
                mb = _fold_rows(s_ref[h, 0:CHUNK, :], jnp.max)
                for c in range(1, n_sub):
                    mb = jnp.maximum(mb, _fold_rows(s_ref[h, c * CHUNK:(c + 1) * CHUNK, :], jnp.max))
                m_new = jnp.maximum(m_old, jnp.max(mb, axis=0, keepdims=True))
                alpha = jnp.exp2(m_old - m_new)
                lsum = jnp.zeros((SUBLANES, tq), F32)
                for c in range(n_sub):
                    rows = slice(c * CHUNK, (c + 1) * CHUNK)
                    p = jnp.exp2(s_ref[h, rows, :] - m_new)
                    lsum = lsum + _fold_rows(p, jnp.sum)
                    p_ref[h, rows, :] = p.astype(BF16)
                l_ref[h:h + 1, :] = alpha * l_ref[h:h + 1, :] + jnp.sum(lsum, axis=0, keepdims=True)
                m_ref[h:h + 1, :] = m_new
                vt = vt_ref[0, h // group, 0:HEAD_DIM, pl.ds(ks, tk)]
                acc_ref[h] = alpha * acc_ref[h] + _dot(vt, p_ref[h])
            return carry

        lax.fori_loop(0, n_kb, body, 0)
        write_output(lambda h: acc_ref[h] * (1.0 / l_ref[h:h + 1, :]))


def _attn_call(shift, q, iq, iwt, klo, khi, vt, iklo, ikhi, B, T, tq):
    N, d_q = q.shape
    nq = T // tq
    top_k = min(TOPK_MAX, T // 4)
    row = lambda w: pl.BlockSpec((tq, w), lambda b, i: (b * nq + i, 0))
    per_batch = lambda shape: pl.BlockSpec((1,) + shape[1:], lambda b, i: (b,) + (0,) * (len(shape) - 1))
    return pl.pallas_call(
        functools.partial(_attn_kernel, top_k=top_k),
        grid=(B, nq),
        in_specs=[pl.BlockSpec(memory_space=pltpu.SMEM),
                  row(d_q), row(iq.shape[1]),
                  pl.BlockSpec((1, IDX_HEADS, tq), lambda b, i: (b, 0, i)),
                  per_batch(klo.shape), per_batch(khi.shape), per_batch(vt.shape),
                  per_batch(iklo.shape), per_batch(ikhi.shape)],
        out_specs=row(d_q),
        out_shape=jax.ShapeDtypeStruct((N, d_q), BF16),
        scratch_shapes=[pltpu.VMEM((T, tq), F32),
                        pltpu.VMEM((tq, tq), F32),
                        pltpu.VMEM((N_HEADS, tq, tq), F32),
                        pltpu.VMEM((N_HEADS, tq, tq), BF16),
                        pltpu.VMEM((N_HEADS, VT_ROWS, tq), F32),
                        pltpu.VMEM((N_HEADS, tq), F32),
                        pltpu.VMEM((N_HEADS, tq), F32),
                        pltpu.VMEM((N_HEADS, HEAD_DIM, tq), F32)],
        compiler_params=pltpu.CompilerParams(
            dimension_semantics=("arbitrary", "arbitrary"),
            vmem_limit_bytes=VMEM_LIMIT_BYTES),
        name="dsa",
    )(shift, q, iq, iwt, klo, khi, vt, iklo, ikhi)


def _mlp_kernel(x_ref, gates_ref, yr_ref, ya_ref, wor_ref, woa_ref, wout_ref, n2_ref,
                wfi_ref, wfo_ref, o_ref, *, ff_chunk):
    d = x_ref.shape[1]
    merged = (gates_ref[:, 0:d].astype(F32) * _dot(yr_ref[...], wor_ref[...])
              + gates_ref[:, d:2 * d].astype(F32) * _dot(ya_ref[...], woa_ref[...]))
    x1 = x_ref[...] + _dot(merged.astype(BF16), wout_ref[...])
    ms = jnp.mean(x1 * x1, axis=-1, keepdims=True)
    h2 = (x1 * lax.rsqrt(ms + EPS) * n2_ref[...]).astype(BF16)
    acc = x1
    for c in range(wfi_ref.shape[1] // ff_chunk):
        sl = slice(c * ff_chunk, (c + 1) * ff_chunk)
        hid = jnp.square(jax.nn.relu(_dot(h2, wfi_ref[:, sl])))
        acc = acc + _dot(hid.astype(BF16), wfo_ref[sl, :])
    o_ref[...] = acc


def _mlp_call(x2d, gates, y_rnn, y_attn, params, tm):
    N, D = x2d.shape
    row = lambda w: pl.BlockSpec((tm, w), lambda i: (i, 0))
    return pl.pallas_call(
        functools.partial(_mlp_kernel, ff_chunk=1024),
        grid=(N // tm,),
        in_specs=[row(D), row(gates.shape[1]), row(D), row(D)]
        + [_const_spec(p.shape) for p in params],
        out_specs=row(D),
        out_shape=jax.ShapeDtypeStruct((N, D), F32),
        compiler_params=pltpu.CompilerParams(
            dimension_semantics=("arbitrary",),
            vmem_limit_bytes=VMEM_LIMIT_BYTES),
        name="merge_mlp",
    )(x2d, gates, y_rnn, y_attn, *params)


def _block_diag_slabs(wa, wx, slab):
    nb, bd, _ = wa.shape
    per = slab // bd

    def diag(w):
        w = w.reshape(nb // per, per, bd, bd)
        eye = jnp.eye(per, dtype=w.dtype)
        return jnp.einsum('spde,pq->spdqe', w, eye).reshape(nb // per, slab, slab)

    return jnp.concatenate([diag(wa), diag(wx)], axis=-1)


def _layer(x, norm1_w, w_in, conv_w, conv_b, rg_wa, rg_ba, rg_wx, rg_bx, rg_lambda,
           q_norm_w, k_norm_w, idx_k_norm_w, w_o_rnn, w_o_attn, gate_b, w_out,
           norm2_w, w_ff_in, w_ff_out):
    B, T, D = x.shape
    d_rnn = conv_w.shape[1]
    d_attn = N_HEADS * HEAD_DIM
    d_kv = N_KV_HEADS * HEAD_DIM
    d_iq = IDX_HEADS * IDX_DIM
    sizes = (d_rnn, d_rnn, d_attn, d_kv, d_kv, d_iq, IDX_DIM, IDX_HEADS, 2 * D)
    offs = [0]
    for s in sizes:
        offs.append(offs[-1] + s)
    col = lambda i: w_in[:, offs[i]:offs[i + 1]]
    row = lambda v: v.reshape(1, -1).astype(F32)

    proj_params = (
        row(norm1_w),
        jnp.concatenate([col(0), col(1)], axis=1).astype(BF16),
        col(8).astype(BF16),
        col(2).astype(BF16),
        col(3).astype(BF16),
        col(5).astype(BF16),
        jnp.pad(col(6), ((0, 0), (0, LANES - IDX_DIM))).astype(BF16),
        col(4).T.astype(BF16),
        col(7).T.astype(BF16),
        row(gate_b),
        row(jnp.tile(q_norm_w, LANES // HEAD_DIM)) * (HEAD_DIM ** -0.5 * LOG2_E),
        row(jnp.tile(k_norm_w, LANES // HEAD_DIM)),
        row(jnp.pad(idx_k_norm_w, (0, LANES - IDX_DIM))),
    )
    x2d = x.reshape(B * T, D)
    rg, gates, q, klo, khi, vt, iq, iklo, ikhi, iwt = _proj_call(x2d, proj_params, B, T, tm=512)

    rnn_params = (conv_w.astype(F32), row(conv_b),
                  _block_diag_slabs(rg_wa, rg_wx, 2 * LANES).astype(BF16),
                  row(rg_ba), row(rg_bx), row(rg_lambda))
    y_rnn = _rnn_call(rg, rnn_params, B, T, tb=256)

    shift = (SHIFT_MARGIN * HEAD_DIM * jnp.max(jnp.abs(proj_params[10]))
             * jnp.max(jnp.abs(k_norm_w))).reshape(1).astype(F32)
    y_attn = _attn_call(shift, q, iq, iwt, klo, khi, vt, iklo, ikhi, B, T, tq=256)

    mlp_params = (w_o_rnn.astype(BF16), w_o_attn.astype(BF16), w_out.astype(BF16), row(norm2_w),
                  w_ff_in.astype(BF16), w_ff_out.astype(BF16))
    out = _mlp_call(x2d, gates, y_rnn, y_attn, mlp_params, tm=256)
    return out.reshape(B, T, D)


def kernel(x, norm1_w, w_in, conv_w, conv_b, rg_wa, rg_ba, rg_wx, rg_bx, rg_lambda, q_norm_w,
           k_norm_w, idx_k_norm_w, w_o_rnn, w_o_attn, gate_b, w_out, norm2_w, w_ff_in, w_ff_out):
    for l in range(norm1_w.shape[0]):
        x = _layer(x, norm1_w[l], w_in[l], conv_w[l], conv_b[l], rg_wa[l], rg_ba[l], rg_wx[l],
                   rg_bx[l], rg_lambda[l], q_norm_w[l], k_norm_w[l], idx_k_norm_w[l], w_o_rnn[l],
                   w_o_attn[l], gate_b[l], w_out[l], norm2_w[l], w_ff_in[l], w_ff_out[l])
    return x
```

```python
import functools

import jax
import jax.numpy as jnp
from jax import lax
from jax.experimental import pallas as pl
from jax.experimental.pallas import tpu as pltpu

CHUNK = 64
RNN_BLOCKS = 16
CONV_WIDTH = 4
LRU_C = 8.0
N_HEADS = 16
N_KV_HEADS = 4
HEAD_DIM = 64
IDX_HEADS = 8
IDX_DIM = 64
TOPK_MAX = 256
EPS = 1e-6

LANES = 128
SUBLANES = 8
VMEM_LIMIT_BYTES = 56 * 1024 * 1024

MASKED = -1e30
LOG2_E = 1.4426950408889634
SHIFT_MARGIN = 1.02
SOFTMAX_SHIFT_LIMIT = 50.0
BF16_SUBLANES = 16
VT_ROWS = HEAD_DIM + BF16_SUBLANES
BISECT_STEPS_PER_ROUND = 4
COUNT_ACCUMULATORS = 4
BISECT_MAX_ROUNDS = 8

BF16 = jnp.bfloat16
F32 = jnp.float32


def _dot(a, b):
    return jnp.dot(a, b, preferred_element_type=F32)


def _dot_nt(a, b):
    return lax.dot_general(a, b, (((1,), (1,)), ((), ())), preferred_element_type=F32)


def _segment_rms_scale(x, seg):
    x2 = x * x
    lane = lax.broadcasted_iota(jnp.int32, x.shape, 1)
    lo = lane < seg
    s_lo = jnp.sum(jnp.where(lo, x2, 0.0), axis=-1, keepdims=True)
    s_hi = jnp.sum(x2, axis=-1, keepdims=True) - s_lo
    r_lo = lax.rsqrt(s_lo * (1.0 / seg) + EPS)
    r_hi = lax.rsqrt(s_hi * (1.0 / seg) + EPS)
    return jnp.where(lo, r_lo, r_hi)


def _proj_kernel(x_ref, n1_ref, wrg_ref, wg_ref, wq_ref, wk_ref, wiq_ref, wik_ref, wvt_ref,
                 wiwt_ref, gb_ref, qnw_ref, knw_ref, iknw_ref,
                 rg_ref, gates_ref, q_ref, klo_ref, khi_ref, vt_ref, iq_ref, iklo_ref, ikhi_ref,
                 iwt_ref, *, idx_scale):
    x = x_ref[...]
    ms = jnp.mean(x * x, axis=-1, keepdims=True)
    h = (x * lax.rsqrt(ms + EPS) * n1_ref[...]).astype(BF16)

    rg_ref[...] = _dot(h, wrg_ref[...]).astype(BF16)
    gates_ref[...] = jax.nn.sigmoid(_dot(h, wg_ref[...]) + gb_ref[...]).astype(BF16)
    iq_ref[...] = _dot(h, wiq_ref[...]).astype(BF16)

    q = _dot(h, wq_ref[...])
    for j in range(q.shape[1] // LANES):
        sl = slice(j * LANES, (j + 1) * LANES)
        qj = q[:, sl]
        q_ref[:, sl] = (qj * _segment_rms_scale(qj, HEAD_DIM) * qnw_ref[...]).astype(BF16)

    k = _dot(h, wk_ref[...])
    lane = lax.broadcasted_iota(jnp.int32, (k.shape[0], LANES), 1)
    lo = lane < HEAD_DIM
    for j in range(k.shape[1] // LANES):
        kj = k[:, j * LANES:(j + 1) * LANES]
        kj = kj * _segment_rms_scale(kj, HEAD_DIM) * knw_ref[...]
        ks = pltpu.roll(kj, HEAD_DIM, axis=1)
        klo_ref[0, 2 * j] = jnp.where(lo, kj, 0.0).astype(BF16)
        khi_ref[0, 2 * j] = jnp.where(lo, 0.0, ks).astype(BF16)
        klo_ref[0, 2 * j + 1] = jnp.where(lo, ks, 0.0).astype(BF16)
        khi_ref[0, 2 * j + 1] = jnp.where(lo, 0.0, kj).astype(BF16)

    ik = _dot(h, wik_ref[...])
    ik_ms = jnp.sum(ik * ik, axis=-1, keepdims=True) * (1.0 / IDX_DIM)
    ik = ik * lax.rsqrt(ik_ms + EPS) * iknw_ref[...]
    iklo_ref[0] = ik.astype(BF16)
    ikhi_ref[0] = pltpu.roll(ik, IDX_DIM, axis=1).astype(BF16)

    vt = _dot_nt(wvt_ref[...], h).astype(BF16)
    ones = jnp.ones((VT_ROWS - HEAD_DIM, vt.shape[1]), BF16)
    for g in range(N_KV_HEADS):
        vt_ref[0, g] = jnp.concatenate([vt[g * HEAD_DIM:(g + 1) * HEAD_DIM, :], ones], axis=0)
    iwt_ref[0] = _dot_nt(wiwt_ref[...], h) * idx_scale


def _const_spec(shape):
    nd = len(shape)
    return pl.BlockSpec(shape, lambda *_: (0,) * nd)


def _proj_call(x2d, params, B, T, tm):
    (n1, wrg, wg, wq, wk, wiq, wik, wvt, wiwt, gb, qnw, knw, iknw) = params
    N, D = x2d.shape
    nt = T // tm
    d_rg, d_g, d_q, d_k, d_iq = wrg.shape[1], wg.shape[1], wq.shape[1], wk.shape[1], wiq.shape[1]
    idx_scale = (IDX_DIM ** -0.5) * (IDX_HEADS ** -0.5)

    row = lambda w: pl.BlockSpec((tm, w), lambda b, i: (b * nt + i, 0))
    in_specs = [row(D)] + [_const_spec(p.shape) for p in params]
    out_shape = (
        jax.ShapeDtypeStruct((N, d_rg), BF16),
        jax.ShapeDtypeStruct((N, d_g), BF16),
        jax.ShapeDtypeStruct((N, d_q), BF16),
        jax.ShapeDtypeStruct((B, N_KV_HEADS, T, LANES), BF16),
        jax.ShapeDtypeStruct((B, N_KV_HEADS, T, LANES), BF16),
        jax.ShapeDtypeStruct((B, N_KV_HEADS, VT_ROWS, T), BF16),
        jax.ShapeDtypeStruct((N, d_iq), BF16),
        jax.ShapeDtypeStruct((B, T, LANES), BF16),
        jax.ShapeDtypeStruct((B, T, LANES), BF16),
        jax.ShapeDtypeStruct((B, IDX_HEADS, T), F32),
    )
    kv_spec = pl.BlockSpec((1, N_KV_HEADS, tm, LANES), lambda b, i: (b, 0, i, 0))
    ik_spec = pl.BlockSpec((1, tm, LANES), lambda b, i: (b, i, 0))
    out_specs = (
        row(d_rg), row(d_g), row(d_q), kv_spec, kv_spec,
        pl.BlockSpec((1, N_KV_HEADS, VT_ROWS, tm), lambda b, i: (b, 0, 0, i)),
        row(d_iq), ik_spec, ik_spec,
        pl.BlockSpec((1, IDX_HEADS, tm), lambda b, i: (b, 0, i)),
    )
    return pl.pallas_call(
        functools.partial(_proj_kernel, idx_scale=idx_scale),
        grid=(B, nt),
        in_specs=in_specs,
        out_specs=out_specs,
        out_shape=out_shape,
        compiler_params=pltpu.CompilerParams(
            dimension_semantics=("arbitrary", "arbitrary"),
            vmem_limit_bytes=VMEM_LIMIT_BYTES),
        name="proj",
    )(x2d, *params)


def _shift_rows(x, s, fill):
    rolled = pltpu.roll(x, s, axis=0)
    rows = lax.broadcasted_iota(jnp.int32, x.shape, 0)
    return jnp.where(rows >= s, rolled, fill)


def _linear_scan(a, u):
    tb = a.shape[0]
    s = 1
    while s < tb:
        if s < SUBLANES:
            a_s = _shift_rows(a, s, 1.0)
            u_s = _shift_rows(u, s, 0.0)
            u = a * u_s + u
            a = a * a_s
        else:
            u = jnp.concatenate([u[:s], a[s:] * u[:-s] + u[s:]], axis=0)
            a = jnp.concatenate([a[:s], a[s:] * a[:-s]], axis=0)
        s *= 2
    return a, u


def _rnn_kernel(rg_ref, cw_ref, cb_ref, wd_ref, ba_ref, bx_ref, lam_ref, y_ref,
                xbuf_ref, hc_ref):
    tb = y_ref.shape[0]
    d = y_ref.shape[1]
    halo = SUBLANES

    @pl.when(pl.program_id(1) == 0)
    def _():
        xbuf_ref[0:halo, :] = jnp.zeros((halo, d), F32)
        hc_ref[...] = jnp.zeros_like(hc_ref)

    x = rg_ref[:, 0:d].astype(F32)
    xbuf_ref[halo:halo + tb, :] = x
    xc = cb_ref[...] + cw_ref[CONV_WIDTH - 1:CONV_WIDTH, :] * x
    for j in range(CONV_WIDTH - 1):
        off = halo - (CONV_WIDTH - 1) + j
        xc = xc + cw_ref[j:j + 1, :] * xbuf_ref[off:off + tb, :]
    xbuf_ref[0:halo, :] = x[tb - halo:, :]

    xcb = xc.astype(BF16)
    n_slab = wd_ref.shape[0]
    sw = d // n_slab
    pre_a, pre_i = [], []
    for j in range(n_slab):
        ga = _dot(xcb[:, j * sw:(j + 1) * sw], wd_ref[j])
        pre_a.append(ga[:, :sw])
        pre_i.append(ga[:, sw:])
    r = jax.nn.sigmoid(jnp.concatenate(pre_a, axis=1) + ba_ref[...])
    ig = jax.nn.sigmoid(jnp.concatenate(pre_i, axis=1) + bx_ref[...])
    log_a = (-LRU_C) * r * jax.nn.softplus(-lam_ref[...])
    a = jnp.exp(log_a)
    mult = jnp.sqrt(1.0 - a * a)
    u = mult * (ig * xc)

    a_cum, h = _linear_scan(a, u)
    h = h + a_cum * hc_ref[...]
    hc_ref[...] = h[tb - 1:tb, :]

    gate = rg_ref[:, d:2 * d].astype(F32)
    y_ref[...] = (h * jax.nn.gelu(gate)).astype(BF16)


def _rnn_call(rg, params, B, T, tb):
    cw, cb, wd, ba, bx, lam = params
    N = rg.shape[0]
    d = rg.shape[1] // 2
    nt = T // tb
    return pl.pallas_call(
        _rnn_kernel,
        grid=(B, nt),
        in_specs=[pl.BlockSpec((tb, 2 * d), lambda b, i: (b * nt + i, 0))]
        + [_const_spec(p.shape) for p in params],
        out_specs=pl.BlockSpec((tb, d), lambda b, i: (b * nt + i, 0)),
        out_shape=jax.ShapeDtypeStruct((N, d), BF16),
        scratch_shapes=[pltpu.VMEM((tb + SUBLANES, d), F32), pltpu.VMEM((1, d), F32)],
        compiler_params=pltpu.CompilerParams(
            dimension_semantics=("arbitrary", "arbitrary"),
            vmem_limit_bytes=VMEM_LIMIT_BYTES),
        name="rglru",
    )(rg, *params)


def _fold_rows(x, op):
    return op(x.reshape(x.shape[0] // SUBLANES, SUBLANES, x.shape[1]), axis=0)


def _attn_kernel(shift_ref, q_ref, iq_ref, iwt_ref, klo_ref, khi_ref, vt_ref, iklo_ref, ikhi_ref,
                 o_ref, sc_ref, bias_ref, s_ref, p_ref, accv_ref, m_ref, l_ref, acc_ref, *, top_k):
    tq = q_ref.shape[0]
    tk = tq
    qi = pl.program_id(1)
    n_pairs = q_ref.shape[1] // LANES
    group = N_HEADS // N_KV_HEADS
    n_sub = tk // CHUNK

    def idx_block(kb, diagonal):
        ks = pl.multiple_of(kb * tk, tk)
        ik_lo = iklo_ref[0, pl.ds(ks, tk), :]
        ik_hi = ikhi_ref[0, pl.ds(ks, tk), :]
        score = jnp.zeros((tk, tq), F32)
        for p in range(iq_ref.shape[1] // LANES):
            iqp = iq_ref[:, p * LANES:(p + 1) * LANES]
            score = score + jax.nn.relu(_dot_nt(ik_lo, iqp)) * iwt_ref[0, 2 * p:2 * p + 1, :]
            score = score + jax.nn.relu(_dot_nt(ik_hi, iqp)) * iwt_ref[0, 2 * p + 1:2 * p + 2, :]
        if diagonal:
            kc = lax.broadcasted_iota(jnp.int32, (tk, tq), 0) // CHUNK
            qc = lax.broadcasted_iota(jnp.int32, (tk, tq), 1) // CHUNK
            visible = kc <= qc
            sc_ref[pl.ds(ks, tk), :] = jnp.where(visible, score, -jnp.inf)
            return (_fold_rows(jnp.where(visible, score, jnp.inf), jnp.min),
                    _fold_rows(jnp.where(visible, score, -jnp.inf), jnp.max))
        sc_ref[pl.ds(ks, tk), :] = score
        return _fold_rows(score, jnp.min), _fold_rows(score, jnp.max)

    def idx_body(kb, carry):
        mn, mx = idx_block(kb, False)
        return jnp.minimum(carry[0], mn), jnp.maximum(carry[1], mx)

    mn, mx = lax.fori_loop(0, qi, idx_body, (jnp.full((SUBLANES, tq), jnp.inf, F32),
                                             jnp.full((SUBLANES, tq), -jnp.inf, F32)))
    mn_d, mx_d = idx_block(qi, True)
    lo0 = jnp.min(jnp.minimum(mn, mn_d), axis=0, keepdims=True)
    hi0 = jnp.max(jnp.maximum(mx, mx_d), axis=0, keepdims=True)
    hi0 = hi0 + (jnp.abs(hi0) * 2.0 ** -20 + 1e-30)

    row_pos = lax.broadcasted_iota(jnp.int32, (SUBLANES, tq), 0)

    def count(pred, trips):
        def body(kb, acc):
            ks = pl.multiple_of(kb * tk, tk)
            acc = list(acc)
            for r in range(tk // SUBLANES):
                s = sc_ref[pl.ds(ks + r * SUBLANES, SUBLANES), :]
                a = acc[r % COUNT_ACCUMULATORS]
                hit = pred(s, row_pos + (ks + r * SUBLANES))
                acc[r % COUNT_ACCUMULATORS] = jnp.where(hit, a + 1, a)
            return tuple(acc)
        zero = jnp.zeros((SUBLANES, tq), jnp.int32)
        acc = lax.fori_loop(0, trips, body, (zero,) * COUNT_ACCUMULATORS)
        return jnp.sum(functools.reduce(jnp.add, acc), axis=0, keepdims=True)

    def any_row(flag):
        return (jnp.max(jnp.where(flag, 1.0, 0.0)) > 0.0).astype(jnp.int32)

    n_kb = qi + 1
    n_visible = qi * tq + (lax.broadcasted_iota(jnp.int32, (1, tq), 1) // CHUNK + 1) * CHUNK

    def bisect_round(state):
        lo, hi, c_lo, _, rounds = state
        for _ in range(BISECT_STEPS_PER_ROUND):
            mid = lo + 0.5 * (hi - lo)
            c = count(lambda s, pos: s >= mid, n_kb)
            ok = c >= top_k
            lo, hi, c_lo = jnp.where(ok, mid, lo), jnp.where(ok, hi, mid), jnp.where(ok, c, c_lo)
        return lo, hi, c_lo, any_row(c_lo > top_k), rounds + 1

    lo, hi, c_lo, _, _ = lax.while_loop(
        lambda st: (st[3] > 0) & (st[4] < BISECT_MAX_ROUNDS), bisect_round,
        (lo0, hi0, n_visible, any_row(n_visible > top_k), jnp.int32(0)))

    def snap(state):
        lo, hi, c_lo, todo, _ = state
        def below_hi(kb, best):
            ks = pl.multiple_of(kb * tk, tk)
            s = sc_ref[pl.ds(ks, tk), :]
            return jnp.maximum(best, _fold_rows(jnp.where(s < hi, s, -jnp.inf), jnp.max))
        v = jnp.max(lax.fori_loop(0, n_kb, below_hi, jnp.full((SUBLANES, tq), -jnp.inf, F32)),
                    axis=0, keepdims=True)
        c_v = count(lambda s, pos: s >= v, n_kb)
        found = (todo > 0) & (c_v >= top_k)
        lo, c_lo = jnp.where(found, v, lo), jnp.where(found, c_v, c_lo)
        hi = jnp.where((todo > 0) & (c_v < top_k), v, hi)
        todo = jnp.where(found, 0, todo)
        return lo, hi, c_lo, todo, any_row(todo > 0)

    unsettled = jnp.where(c_lo > top_k, 1, 0)
    thr, _, c_thr, _, _ = lax.while_loop(lambda st: st[4] > 0, snap,
                                         (lo, hi, c_lo, unsettled, any_row(unsettled > 0)))

    tied = c_thr > top_k
    tie_trips = any_row(tied)
    need = top_k - count(lambda s, pos: s > thr, tie_trips * n_kb)
    n_bits = (sc_ref.shape[0] - 1).bit_length()

    def last_bit(i, last):
        cand = last + jnp.left_shift(jnp.int32(1), n_bits - 1 - i)
        before = count(lambda s, pos: (s == thr) & (pos < cand), n_kb)
        return jnp.where(before < need, cand, last)

    last = lax.fori_loop(0, tie_trips * n_bits, last_bit, jnp.zeros((1, tq), jnp.int32))
    last = jnp.where(tied, last, sc_ref.shape[0])

    def selection_bias(ks, selected_value):
        sc = sc_ref[pl.ds(ks, tk), :]
        pos = ks + lax.broadcasted_iota(jnp.int32, (tk, tq), 0)
        tie_value = jnp.where((sc == thr) & (pos <= last), selected_value, MASKED)
        return jnp.where(sc > thr, selected_value, tie_value)

    def all_head_scores(ks):
        for j in range(n_pairs):
            qp = q_ref[:, j * LANES:(j + 1) * LANES]
            g = (2 * j) // group
            for half in range(2):
                kx = (klo_ref if half == 0 else khi_ref)[0, g, pl.ds(ks, tk), :]
                s_ref[2 * j + half] = _dot_nt(kx, qp) + bias_ref[...]

    def write_output(head_out):
        for j in range(n_pairs):
            pair = jnp.concatenate([head_out(2 * j), head_out(2 * j + 1)], axis=0)
            o_ref[:, j * LANES:(j + 1) * LANES] = pair.T.astype(BF16)

    shift = shift_ref[0]

    @pl.when(shift <= SOFTMAX_SHIFT_LIMIT)
    def _fixed_shift():
        accv_ref[...] = jnp.zeros(accv_ref.shape, F32)

        def body(kb, carry):
            ks = pl.multiple_of(kb * tk, tk)
            bias_ref[...] = selection_bias(ks, -shift)
            all_head_scores(ks)
            for h in range(N_HEADS):
                for c in range(n_sub):
                    rows = slice(c * CHUNK, (c + 1) * CHUNK)
                    p_ref[h, rows, :] = jnp.exp2(s_ref[h, rows, :]).astype(BF16)
            for h in range(N_HEADS):
                accv_ref[h] = accv_ref[h] + _dot(vt_ref[0, h // group, :, pl.ds(ks, tk)], p_ref[h])
            return carry

        lax.fori_loop(0, n_kb, body, 0)
        write_output(lambda h: accv_ref[h, 0:HEAD_DIM, :] * (1.0 / accv_ref[h, HEAD_DIM:HEAD_DIM + 1, :]))

    @pl.when(shift > SOFTMAX_SHIFT_LIMIT)
    def _running_max():
        m_ref[...] = jnp.full(m_ref.shape, MASKED, F32)
        l_ref[...] = jnp.zeros(l_ref.shape, F32)
        acc_ref[...] = jnp.zeros(acc_ref.shape, F32)

        def body(kb, carry):
            ks = pl.multiple_of(kb * tk, tk)
            bias_ref[...] = selection_bias(ks, 0.0)
            all_head_scores(ks)
            for h in range(N_HEADS):
                m_old = m_ref[h:h + 1, :]
                mb = _fold_rows(s_ref[h, 0:CHUNK, :], jnp.max)
                for c in range(1, n_sub):
                    mb = jnp.maximum(mb, _fold_rows(s_ref[h, c * CHUNK:(c + 1) * CHUNK, :], jnp.max))
                m_new = jnp.maximum(m_old, jnp.max(mb, axis=0, keepdims=True))
                alpha = jnp.exp2(m_old - m_new)
                lsum = jnp.zeros((SUBLANES, tq), F32)
                for c in range(n_sub):
                    rows = slice(c * CHUNK, (c + 1) * CHUNK)
                    p = jnp.exp2(s_ref[h, rows, :] - m_new)
                    lsum = lsum + _fold_rows(p, jnp.sum)
                    p_ref[h, rows, :] = p.astype(BF16)
                l_ref[h:h + 1, :] = alpha * l_ref[h:h + 1, :] + jnp.sum(lsum, axis=0, keepdims=True)
                m_ref[h:h + 1, :] = m_new
                vt = vt_ref[0, h // group, 0:HEAD_DIM, pl.ds(ks, tk)]
                acc_ref[h] = alpha * acc_ref[h] + _dot(vt, p_ref[h])
            return carry

        lax.fori_loop(0, n_kb, body, 0)
        write_output(lambda h: acc_ref[h] * (1.0 / l_ref[h:h + 1, :]))


def _attn_call(shift, q, iq, iwt, klo, khi, vt, iklo, ikhi, B, T, tq):
    N, d_q = q.shape
    nq = T // tq
    top_k = min(TOPK_MAX, T // 4)
    row = lambda w: pl.BlockSpec((tq, w), lambda b, i: (b * nq + i, 0))
    per_batch = lambda shape: pl.BlockSpec((1,) + shape[1:], lambda b, i: (b,) + (0,) * (len(shape) - 1))
    return pl.pallas_call(
        functools.partial(_attn_kernel, top_k=top_k),
        grid=(B, nq),
        in_specs=[pl.BlockSpec(memory_space=pltpu.SMEM),
                  row(d_q), row(iq.shape[1]),
                  pl.BlockSpec((1, IDX_HEADS, tq), lambda b, i: (b, 0, i)),
                  per_batch(klo.shape), per_batch(khi.shape), per_batch(vt.shape),
                  per_batch(iklo.shape), per_batch(ikhi.shape)],
        out_specs=row(d_q),
        out_shape=jax.ShapeDtypeStruct((N, d_q), BF16),
        scratch_shapes=[pltpu.VMEM((T, tq), F32),
                        pltpu.VMEM((tq, tq), F32),
                        pltpu.VMEM((N_HEADS, tq, tq), F32),
                        pltpu.VMEM((N_HEADS, tq, tq), BF16),
                        pltpu.VMEM((N_HEADS, VT_ROWS, tq), F32),
                        pltpu.VMEM((N_HEADS, tq), F32),
                        pltpu.VMEM((N_HEADS, tq), F32),
                        pltpu.VMEM((N_HEADS, HEAD_DIM, tq), F32)],
        compiler_params=pltpu.CompilerParams(
            dimension_semantics=("arbitrary", "arbitrary"),
            vmem_limit_bytes=VMEM_LIMIT_BYTES),
        name="dsa",
    )(shift, q, iq, iwt, klo, khi, vt, iklo, ikhi)


def _mlp_kernel(x_ref, gates_ref, yr_ref, ya_ref, wor_ref, woa_ref, wout_ref, n2_ref,
                wfi_ref, wfo_ref, o_ref, *, ff_chunk):
    d = x_ref.shape[1]
    merged = (gates_ref[:, 0:d].astype(F32) * _dot(yr_ref[...], wor_ref[...])
              + gates_ref[:, d:2 * d].astype(F32) * _dot(ya_ref[...], woa_ref[...]))
    x1 = x_ref[...] + _dot(merged.astype(BF16), wout_ref[...])
    ms = jnp.mean(x1 * x1, axis=-1, keepdims=True)
    h2 = (x1 * lax.rsqrt(ms + EPS) * n2_ref[...]).astype(BF16)
    acc = x1
    for c in range(wfi_ref.shape[1] // ff_chunk):
        sl = slice(c * ff_chunk, (c + 1) * ff_chunk)
        hid = jnp.square(jax.nn.relu(_dot(h2, wfi_ref[:, sl])))
        acc = acc + _dot(hid.astype(BF16), wfo_ref[sl, :])
    o_ref[...] = acc


def _mlp_call(x2d, gates, y_rnn, y_attn, params, tm):
    N, D = x2d.shape
    row = lambda w: pl.BlockSpec((tm, w), lambda i: (i, 0))
    return pl.pallas_call(
        functools.partial(_mlp_kernel, ff_chunk=1024),
        grid=(N // tm,),
        in_specs=[row(D), row(gates.shape[1]), row(D), row(D)]
        + [_const_spec(p.shape) for p in params],
        out_specs=row(D),
        out_shape=jax.ShapeDtypeStruct((N, D), F32),
        compiler_params=pltpu.CompilerParams(
            dimension_semantics=("arbitrary",),
            vmem_limit_bytes=VMEM_LIMIT_BYTES),
        name="merge_mlp",
    )(x2d, gates, y_rnn, y_attn, *params)


def _block_diag_slabs(wa, wx, slab):
    nb, bd, _ = wa.shape
    per = slab // bd

    def diag(w):
        w = w.reshape(nb // per, per, bd, bd)
        eye = jnp.eye(per, dtype=w.dtype)
        return jnp.einsum('spde,pq->spdqe', w, eye).reshape(nb // per, slab, slab)

    return jnp.concatenate([diag(wa), diag(wx)], axis=-1)


def _layer(x, norm1_w, w_in, conv_w, conv_b, rg_wa, rg_ba, rg_wx, rg_bx, rg_lambda,
           q_norm_w, k_norm_w, idx_k_norm_w, w_o_rnn, w_o_attn, gate_b, w_out,
           norm2_w, w_ff_in, w_ff_out):
    B, T, D = x.shape
    d_rnn = conv_w.shape[1]
    d_attn = N_HEADS * HEAD_DIM
    d_kv = N_KV_HEADS * HEAD_DIM
    d_iq = IDX_HEADS * IDX_DIM
    sizes = (d_rnn, d_rnn, d_attn, d_kv, d_kv, d_iq, IDX_DIM, IDX_HEADS, 2 * D)
    offs = [0]
    for s in sizes:
        offs.append(offs[-1] + s)
    col = lambda i: w_in[:, offs[i]:offs[i + 1]]
    row = lambda v: v.reshape(1, -1).astype(F32)

    proj_params = (
        row(norm1_w),
        jnp.concatenate([col(0), col(1)], axis=1).astype(BF16),
        col(8).astype(BF16),
        col(2).astype(BF16),
        col(3).astype(BF16),
        col(5).astype(BF16),
        jnp.pad(col(6), ((0, 0), (0, LANES - IDX_DIM))).astype(BF16),
        col(4).T.astype(BF16),
        col(7).T.astype(BF16),
        row(gate_b),
        row(jnp.tile(q_norm_w, LANES // HEAD_DIM)) * (HEAD_DIM ** -0.5 * LOG2_E),
        row(jnp.tile(k_norm_w, LANES // HEAD_DIM)),
        row(jnp.pad(idx_k_norm_w, (0, LANES - IDX_DIM))),
    )
    x2d = x.reshape(B * T, D)
    rg, gates, q, klo, khi, vt, iq, iklo, ikhi, iwt = _proj_call(x2d, proj_params, B, T, tm=512)

    rnn_params = (conv_w.astype(F32), row(conv_b),
                  _block_diag_slabs(rg_wa, rg_wx, 2 * LANES).astype(BF16),
                  row(rg_ba), row(rg_bx), row(rg_lambda))
    y_rnn = _rnn_call(rg, rnn_params, B, T, tb=256)

    shift = (SHIFT_MARGIN * HEAD_DIM * jnp.max(jnp.abs(proj_params[10]))
             * jnp.max(jnp.abs(k_norm_w))).reshape(1).astype(F32)
    y_attn = _attn_call(shift, q, iq, iwt, klo, khi, vt, iklo, ikhi, B, T, tq=256)

    mlp_params = (w_o_rnn.astype(BF16), w_o_attn.astype(BF16), w_out.astype(BF16), row(norm2_w),
                  w_ff_in.astype(BF16), w_ff_out.astype(BF16))
    out = _mlp_call(x2d, gates, y_rnn, y_attn, mlp_params, tm=256)
    return out.reshape(B, T, D)


def kernel(x, norm1_w, w_in, conv_w, conv_b, rg_wa, rg_ba, rg_wx, rg_bx, rg_lambda, q_norm_w,
           k_norm_w, idx_k_norm_w, w_o_rnn, w_o_attn, gate_b, w_out, norm2_w, w_ff_in, w_ff_out):
    for l in range(norm1_w.shape[0]):
        x = _layer(x, norm1_w[l], w_in[l], conv_w[l], conv_b[l], rg_wa[l], rg_ba[l], rg_wx[l],
                   rg_bx[l], rg_lambda[l], q_norm_w[l], k_norm_w[l], idx_k_norm_w[l], w_o_rnn[l],
                   w_o_attn[l], gate_b[l], w_out[l], norm2_w[l], w_ff_in[l], w_ff_out[l])
    return x
```
